```python
import jax, jax.numpy as jnp
from jax import lax
import numpy as np

D_MODEL = 2048
BATCH = 1
SEQ = 8192
DEPTH = 4

N_MIXERS = 4
GROUP_WIDTH = D_MODEL // N_MIXERS
HEAD_DIM = 128
HEADS_PER_GROUP = GROUP_WIDTH // HEAD_DIM
D_FF = 5632
CONV_WIDTH = 3
POOL_WINDOWS = (2, 4, 8, 16)
POOL_GROUPS = len(POOL_WINDOWS)
POOL_GROUP_WIDTH = GROUP_WIDTH // POOL_GROUPS
MLSTM_CHUNK = 128
SB_BLOCK = 128
NORM_EPS = 1e-6

_COL_SIZES = (GROUP_WIDTH,) * 3 + (GROUP_WIDTH,) * 4 + (HEADS_PER_GROUP,) * 2 + (GROUP_WIDTH,) + (GROUP_WIDTH,) * 3
D_IN_PROJ = sum(_COL_SIZES)
SPLIT_POINTS = tuple(int(s) for s in np.cumsum(_COL_SIZES)[:-1])

kernel_name = "hymba_style_parallel_hybrid_macaron"


def rms_norm(x, g):
    xf = x.astype(jnp.float32)
    y = xf * lax.rsqrt(jnp.mean(xf * xf, axis=-1, keepdims=True) + NORM_EPS)
    return (y * g.astype(jnp.float32)).astype(x.dtype)


def swiglu(x, w_gate, w_up, w_down):
    return (jax.nn.silu(x @ w_gate) * (x @ w_up)) @ w_down


def short_conv_mixer(b, c, u, conv_w):
    z = c * u
    y = lax.conv_general_dilated(
        z, conv_w[:, None, :].astype(z.dtype), window_strides=(1,),
        padding=((CONV_WIDTH - 1, 0),), dimension_numbers=('NWC', 'WIO', 'NWC'),
        feature_group_count=z.shape[-1])
    return b * y


def mlstm_mixer(q, k, v, i_pre, f_pre, o_pre, head_gain):
    B_, S, H, dh = q.shape
    L = MLSTM_CHUNK
    NC = S // L
    f32 = jnp.float32
    qf = q.astype(f32)
    kf = k.astype(f32) * (dh ** -0.5)
    vf = v.astype(f32)
    log_f = jax.nn.log_sigmoid(f_pre.astype(f32))
    i_g = i_pre.astype(f32)

    def to_chunks(a):
        return a.reshape(B_, NC, L, H, -1).transpose(1, 0, 3, 2, 4)

    def gate_chunks(a):
        return a.reshape(B_, NC, L, H).transpose(1, 0, 3, 2)

    causal = jnp.tril(jnp.ones((L, L), dtype=bool))

    def step(carry, inp):
        C, n, m = carry
        qb, kb, vb, lf, ig = inp
        b = jnp.cumsum(lf, axis=-1)
        D = jnp.where(causal, b[..., :, None] - b[..., None, :] + ig[..., None, :], -jnp.inf)
        inter = b + m[..., None]
        m_t = jnp.maximum(inter, jnp.max(D, axis=-1))
        w_intra = jnp.exp(D - m_t[..., None])
        w_inter = jnp.exp(inter - m_t)
        s = jnp.einsum('bhtd,bhsd->bhts', qb, kb) * w_intra
        num = jnp.einsum('bhts,bhsd->bhtd', s, vb) + w_inter[..., None] * jnp.einsum('bhtd,bhde->bhte', qb, C)
        den = jnp.sum(s, axis=-1) + w_inter * jnp.einsum('bhtd,bhd->bht', qb, n)
        h = num / jnp.maximum(jnp.abs(den), jnp.exp(-m_t))[..., None]
        b_last = b[..., -1]
        g = b_last[..., None] - b + ig
        m_new = jnp.maximum(b_last + m, jnp.max(g, axis=-1))
        decay = jnp.exp(b_last + m - m_new)
        wk = jnp.exp(g - m_new[..., None])
        C_new = decay[..., None, None] * C + jnp.einsum('bhs,bhsd,bhse->bhde', wk, kb, vb)
        n_new = decay[..., None] * n + jnp.einsum('bhs,bhsd->bhd', wk, kb)
        return (C_new, n_new, m_new), h

    init = (jnp.zeros((B_, H, dh, dh), f32), jnp.zeros((B_, H, dh), f32), jnp.zeros((B_, H), f32))
    _, hs = lax.scan(step, init, (to_chunks(qf), to_chunks(kf), to_chunks(vf), gate_chunks(log_f), gate_chunks(i_g)))
    h = hs.transpose(1, 0, 3, 2, 4).reshape(B_, S, H, dh)
    h = h * lax.rsqrt(jnp.mean(h * h, axis=-1, keepdims=True) + NORM_EPS)
    h = h.reshape(B_, S, H * dh) * head_gain.astype(f32)
    return (jax.nn.sigmoid(o_pre.astype(f32)) * h).astype(q.dtype)


def pool_mixer(u, pool_w, pool_scale):
    B_, S, _ = u.shape
    f32 = jnp.float32
    uf = u.astype(f32).reshape(B_, S, POOL_GROUPS, POOL_GROUP_WIDTH)
    cs = jnp.pad(jnp.cumsum(uf, axis=1), ((0, 0), (1, 0), (0, 0), (0, 0)))
    t = jnp.arange(S)
    win = jnp.array(POOL_WINDOWS, dtype=jnp.int32)
    lo = jnp.maximum(t[:, None] + 1 - win[None, :], 0)
    grp = jnp.arange(POOL_GROUPS)[None, :]
    window_sum = cs[:, 1:] - cs[:, lo, grp]
    count = (t[:, None] + 1 - lo).astype(f32)
    pooled = window_sum / count[None, :, :, None] - uf
    y = jnp.einsum('bsgc,gcd->bsgd', pooled, pool_w.astype(f32)).reshape(B_, S, GROUP_WIDTH)
    return (y * pool_scale.astype(f32)).astype(u.dtype)


def stick_breaking_mixer(q, k, v):
    B_, S, H, dh = q.shape
    NB = S // SB_BLOCK
    f32 = jnp.float32
    qf = q.astype(f32) * (dh ** -0.5)
    kf = k.astype(f32)
    vf = v.astype(f32)
    q_blocks = qf.reshape(B_, NB, SB_BLOCK, H, dh).transpose(1, 0, 3, 2, 4)
    key_pos = jnp.arange(S)

    def block(args):
        q_blk, blk = args
        q_pos = blk * SB_BLOCK + jnp.arange(SB_BLOCK)
        z = jnp.einsum('bhqd,bshd->bhqs', q_blk, kf)
        past = key_pos[None, :] < q_pos[:, None]
        log_1m_beta = jnp.where(past, jax.nn.log_sigmoid(-z), 0.0)
        rest = lax.cumsum(log_1m_beta, axis=3, reverse=True) - log_1m_beta
        A = jnp.where(past, jnp.exp(jax.nn.log_sigmoid(z) + rest), 0.0)
        return jnp.einsum('bhqs,bshd->bqhd', A, vf)

    out = lax.map(block, (q_blocks, jnp.arange(NB)))
    return out.transpose(1, 0, 2, 3, 4).reshape(B_, S, H * dh).astype(q.dtype)


def token_mix(h, w_in, w_out, conv_w, pool_w, pool_scale, i_bias, f_bias, head_gain):
    B_, S, _ = h.shape
    proj = h @ w_in
    (cb, cc, cu, mq, mk, mv, mo, mi, mf, pu, sq, sk, sv) = jnp.split(proj, SPLIT_POINTS, axis=-1)

    def heads(a):
        return a.reshape(B_, S, HEADS_PER_GROUP, HEAD_DIM)

    y_conv = short_conv_mixer(cb, cc, cu, conv_w)
    y_mlstm = mlstm_mixer(heads(mq), heads(mk), heads(mv), mi + i_bias, mf + f_bias, mo, head_gain)
    y_pool = pool_mixer(pu, pool_w, pool_scale)
    y_sb = stick_breaking_mixer(heads(sq), heads(sk), heads(sv))
    y = jnp.concatenate([y_conv.astype(h.dtype), y_mlstm.astype(h.dtype),
                         y_pool.astype(h.dtype), y_sb.astype(h.dtype)], axis=-1)
    return y @ w_out


def setup_inputs(seed: int = 0) -> dict:
    key = jax.random.key(seed)
    ks = jax.random.split(key, 17)
    f32 = jnp.float32
    G, H = GROUP_WIDTH, HEADS_PER_GROUP
    nrm = lambda k, shape, scale: jax.random.normal(k, shape, f32) * scale
    return {
        "x": nrm(ks[0], (BATCH, SEQ, D_MODEL), 1.0),
        "w_in": nrm(ks[1], (DEPTH, D_MODEL, D_IN_PROJ), D_MODEL ** -0.5),
        "w_out": nrm(ks[2], (DEPTH, D_MODEL, D_MODEL), D_MODEL ** -0.5),
        "conv_w": nrm(ks[3], (DEPTH, CONV_WIDTH, G), CONV_WIDTH ** -0.5),
        "pool_w": nrm(ks[4], (DEPTH, POOL_GROUPS, POOL_GROUP_WIDTH, POOL_GROUP_WIDTH), POOL_GROUP_WIDTH ** -0.5),
        "pool_scale": 1.0 + nrm(ks[5], (DEPTH, G), 0.02),
        "mlstm_i_bias": nrm(ks[6], (DEPTH, H), 0.1),
        "mlstm_f_bias": 3.0 + 3.0 * jax.random.uniform(ks[7], (DEPTH, H), f32),
        "mlstm_head_gain": 1.0 + nrm(ks[8], (DEPTH, G), 0.02),
        "ffn1_w_gate": nrm(ks[9], (DEPTH, D_MODEL, D_FF), D_MODEL ** -0.5),
        "ffn1_w_up": nrm(ks[10], (DEPTH, D_MODEL, D_FF), D_MODEL ** -0.5),
        "ffn1_w_down": nrm(ks[11], (DEPTH, D_FF, D_MODEL), D_FF ** -0.5),
        "ffn2_w_gate": nrm(ks[12], (DEPTH, D_MODEL, D_FF), D_MODEL ** -0.5),
        "ffn2_w_up": nrm(ks[13], (DEPTH, D_MODEL, D_FF), D_MODEL ** -0.5),
        "ffn2_w_down": nrm(ks[14], (DEPTH, D_FF, D_MODEL), D_FF ** -0.5),
        "norm_gains": 1.0 + nrm(ks[15], (DEPTH, 6, D_MODEL), 0.02),
    }


def reference(x, w_in, w_out, conv_w, pool_w, pool_scale, mlstm_i_bias, mlstm_f_bias, mlstm_head_gain,
              ffn1_w_gate, ffn1_w_up, ffn1_w_down, ffn2_w_gate, ffn2_w_up, ffn2_w_down, norm_gains):
    for l in range(DEPTH):
        g = norm_gains[l]
        h = rms_norm(x, g[0])
        x = x + 0.5 * rms_norm(swiglu(h, ffn1_w_gate[l], ffn1_w_up[l], ffn1_w_down[l]), g[1])
        h = rms_norm(x, g[2])
        y = token_mix(h, w_in[l], w_out[l], conv_w[l], pool_w[l], pool_scale[l],
                      mlstm_i_bias[l], mlstm_f_bias[l], mlstm_head_gain[l])
        x = x + rms_norm(y, g[3])
        h = rms_norm(x, g[4])
        x = x + 0.5 * rms_norm(swiglu(h, ffn2_w_gate[l], ffn2_w_up[l], ffn2_w_down[l]), g[5])
    return x
```

```python
import functools

import jax
import jax.numpy as jnp
from jax import lax
from jax.experimental import pallas as pl
from jax.experimental.pallas import tpu as pltpu

F32 = jnp.float32
BF16 = jnp.bfloat16

NORM_EPS = 1e-6
HEAD_DIM = 128
HEADS = 4
GROUP = HEADS * HEAD_DIM
CONV_WIDTH = 3
POOL_WINDOWS = (2, 4, 8, 16)
POOL_HALO = 16
MLSTM_CHUNK = 128
LANES = 128
NEG_BIG = -1e30

VMEM_LIMIT = 56 * 1024 * 1024

F32_PANELS = 5
BF16_PANELS = 6
P_CB, P_CC, P_CU, P_MO, P_PU = range(F32_PANELS)
P_MQ, P_MK, P_MV, P_SQ, P_SK, P_SV = range(BF16_PANELS)


def _params(*sem):
    return pltpu.CompilerParams(dimension_semantics=sem, vmem_limit_bytes=VMEM_LIMIT)


def _rms_rows(y, g):
    ms = jnp.mean(y * y, axis=-1, keepdims=True)
    return y * lax.rsqrt(ms + NORM_EPS) * g


def _split3(x):
    hi = x.astype(BF16)
    r1 = x - hi.astype(F32)
    mid = r1.astype(BF16)
    lo = (r1 - mid.astype(F32)).astype(BF16)
    return hi, mid, lo


def _norm_kernel(x_ref, g_ref, h_ref):
    h_ref[...] = _rms_rows(x_ref[...], g_ref[...]).astype(BF16)


def _first_norm(x, g):
    S, D = x.shape
    tm = 256
    return pl.pallas_call(
        _norm_kernel,
        grid=(S // tm,),
        in_specs=[pl.BlockSpec((tm, D), lambda i: (i, 0)),
                  pl.BlockSpec((1, D), lambda i: (0, 0))],
        out_specs=pl.BlockSpec((tm, D), lambda i: (i, 0)),
        out_shape=jax.ShapeDtypeStruct((S, D), BF16),
        compiler_params=_params("arbitrary"),
        name="first_norm",
    )(x, g)


def _ffn_up_kernel(h_ref, wg_ref, wu_ref, o_ref, wg_bf, wu_bf, *, row_chunk):
    @pl.when(pl.program_id(1) == 0)
    def _():
        wg_bf[...] = wg_ref[...].astype(BF16)
        wu_bf[...] = wu_ref[...].astype(BF16)

    def body(r, carry):
        rows = pl.ds(pl.multiple_of(r * row_chunk, row_chunk), row_chunk)
        h = h_ref[rows, :]
        g = jnp.dot(h, wg_bf[...], preferred_element_type=F32)
        u = jnp.dot(h, wu_bf[...], preferred_element_type=F32)
        o_ref[rows, :] = (g / (1.0 + jnp.exp(-g)) * u).astype(BF16)
        return carry

    lax.fori_loop(0, h_ref.shape[0] // row_chunk, body, 0)


def _ffn_up(h, wg, wu, *, tm=1024, tn=512, row_chunk=256):
    S, D = h.shape
    N = wg.shape[1]
    tm = min(tm, S)
    return pl.pallas_call(
        functools.partial(_ffn_up_kernel, row_chunk=min(row_chunk, tm)),
        grid=(N // tn, S // tm),
        in_specs=[pl.BlockSpec((tm, D), lambda j, i: (i, 0)),
                  pl.BlockSpec((D, tn), lambda j, i: (0, j)),
                  pl.BlockSpec((D, tn), lambda j, i: (0, j))],
        out_specs=pl.BlockSpec((tm, tn), lambda j, i: (i, j)),
        out_shape=jax.ShapeDtypeStruct((S, N), BF16),
        scratch_shapes=[pltpu.VMEM((D, tn), BF16), pltpu.VMEM((D, tn), BF16)],
        compiler_params=_params("arbitrary", "arbitrary"),
        name="ffn_up",
    )(h, wg, wu)


def _proj_res_kernel(*refs, n_lhs, res_scale, row_chunk, ep_chunk):
    a_refs = refs[:n_lhs]
    w_ref, x_ref, gp_ref, gn_ref, xo_ref, ho_ref, w_bf = refs[n_lhs:]
    k = pl.program_id(1)
    nk = pl.num_programs(1)
    tm = xo_ref.shape[0]
    w_bf[...] = w_ref[...].astype(BF16)

    def accumulate(a_ref, first):
        def body(r, carry):
            rows = pl.ds(pl.multiple_of(r * row_chunk, row_chunk), row_chunk)
            part = jnp.dot(a_ref[rows, :], w_bf[...], preferred_element_type=F32)
            if first:
                xo_ref[rows, :] = part
            else:
                xo_ref[rows, :] += part
            return carry
        lax.fori_loop(0, tm // row_chunk, body, 0)

    if n_lhs == 1:
        pl.when(k == 0)(lambda: accumulate(a_refs[0], True))
        pl.when(k > 0)(lambda: accumulate(a_refs[0], False))
    else:
        for idx, a_ref in enumerate(a_refs):
            pl.when(k == idx)(functools.partial(accumulate, a_ref, idx == 0))

    @pl.when(k == nk - 1)
    def _():
        def body(r, carry):
            rows = pl.ds(pl.multiple_of(r * ep_chunk, ep_chunk), ep_chunk)
            xn = x_ref[rows, :] + res_scale * _rms_rows(xo_ref[rows, :], gp_ref[...])
            xo_ref[rows, :] = xn
            ho_ref[rows, :] = _rms_rows(xn, gn_ref[...]).astype(BF16)
            return carry
        lax.fori_loop(0, tm // ep_chunk, body, 0)


def _proj_res(lhs, w, x, g_post, g_next, *, res_scale, tm=1024, tk=256, row_chunk=256, ep_chunk=16):
    S, D = x.shape
    tm = min(tm, S)
    n_lhs = len(lhs)
    if n_lhs == 1:
        K = lhs[0].shape[1]
        nk = K // tk
        a_specs = [pl.BlockSpec((tm, tk), lambda i, k: (i, k))]
    else:
        tk = lhs[0].shape[1]
        nk = n_lhs
        a_specs = [pl.BlockSpec((tm, tk), lambda i, k: (i, 0)) for _ in lhs]
    return pl.pallas_call(
        functools.partial(_proj_res_kernel, n_lhs=n_lhs, res_scale=res_scale,
                          row_chunk=min(row_chunk, tm), ep_chunk=ep_chunk),
        grid=(S // tm, nk),
        in_specs=a_specs + [
            pl.BlockSpec((tk, D), lambda i, k: (k, 0)),
            pl.BlockSpec((tm, D), lambda i, k: (i, 0), pipeline_mode=pl.Buffered(1)),
            pl.BlockSpec((1, D), lambda i, k: (0, 0)),
            pl.BlockSpec((1, D), lambda i, k: (0, 0))],
        out_specs=[pl.BlockSpec((tm, D), lambda i, k: (i, 0)),
                   pl.BlockSpec((tm, D), lambda i, k: (i, 0))],
        out_shape=[jax.ShapeDtypeStruct((S, D), F32), jax.ShapeDtypeStruct((S, D), BF16)],
        scratch_shapes=[pltpu.VMEM((tk, D), BF16)],
        compiler_params=_params("arbitrary", "arbitrary"),
        name="proj_res",
    )(*lhs, w, x, g_post, g_next)


def _in_proj_kernel(h_ref, w_ref, wg_ref, wgt_ref, of_ref, ob_ref, g_ref, gt_ref, *, scale):
    h = h_ref[...]
    for p in range(F32_PANELS):
        cols = slice(p * GROUP, (p + 1) * GROUP)
        of_ref[:, cols] = jnp.dot(h, w_ref[:, cols], preferred_element_type=F32)
    for p in range(BF16_PANELS):
        cols = slice(p * GROUP, (p + 1) * GROUP)
        wcols = slice((F32_PANELS + p) * GROUP, (F32_PANELS + p + 1) * GROUP)
        r = jnp.dot(h, w_ref[:, wcols], preferred_element_type=F32)
        if p in (P_MK, P_SQ):
            r = r * scale
        ob_ref[:, cols] = r.astype(BF16)
    g_ref[...] = jnp.dot(h, wg_ref[...], preferred_element_type=F32)
    gt_ref[...] = lax.dot_general(wgt_ref[...], h, (((1,), (1,)), ((), ())), preferred_element_type=F32)


def _in_proj(h, w_main, w_gate, w_gate_t, *, tm=512):
    S, D = h.shape
    tm = min(tm, S)
    nf, nb = F32_PANELS * GROUP, BF16_PANELS * GROUP
    return pl.pallas_call(
        functools.partial(_in_proj_kernel, scale=HEAD_DIM ** -0.5),
        grid=(S // tm,),
        in_specs=[pl.BlockSpec((tm, D), lambda i: (i, 0)),
                  pl.BlockSpec((D, nf + nb), lambda i: (0, 0), pipeline_mode=pl.Buffered(1)),
                  pl.BlockSpec((D, LANES), lambda i: (0, 0)),
                  pl.BlockSpec((LANES, D), lambda i: (0, 0))],
        out_specs=[pl.BlockSpec((tm, nf), lambda i: (i, 0)),
                   pl.BlockSpec((tm, nb), lambda i: (i, 0)),
                   pl.BlockSpec((tm, LANES), lambda i: (i, 0)),
                   pl.BlockSpec((LANES, tm), lambda i: (0, i))],
        out_shape=[jax.ShapeDtypeStruct((S, nf), F32),
                   jax.ShapeDtypeStruct((S, nb), BF16),
                   jax.ShapeDtypeStruct((S, LANES), F32),
                   jax.ShapeDtypeStruct((LANES, S), F32)],
        compiler_params=_params("arbitrary"),
        name="in_proj",
    )(h, w_main, w_gate, w_gate_t)


def _conv_pool_kernel(cb_ref, cc_ref, cu_ref, pu_ref, cch_ref, cuh_ref, puh_ref,
                      cw_ref, pw_ref, ps_ref, yc_ref, yp_ref, zbuf, pbuf):
    i = pl.program_id(0)
    tm = cb_ref.shape[0]
    keep = (i > 0).astype(F32)
    H = POOL_HALO

    zbuf[0:H, :] = cch_ref[...] * cuh_ref[...] * keep
    zbuf[H:H + tm, :] = cc_ref[...] * cu_ref[...]
    y = cw_ref[0:1, :] * zbuf[H - 2:H - 2 + tm, :]
    y = y + cw_ref[1:2, :] * zbuf[H - 1:H - 1 + tm, :]
    y = y + cw_ref[2:3, :] * zbuf[H:H + tm, :]
    yc_ref[...] = (cb_ref[...] * y).astype(BF16)

    pbuf[0:H, :] = puh_ref[...] * keep
    pbuf[H:H + tm, :] = pu_ref[...]
    t = i * tm + lax.broadcasted_iota(jnp.int32, (tm, 1), 0)
    for g, win in enumerate(POOL_WINDOWS):
        cols = slice(g * LANES, (g + 1) * LANES)
        u = pbuf[H:H + tm, cols]
        wsum = u
        for d in range(1, win):
            wsum = wsum + pbuf[H - d:H - d + tm, cols]
        count = jnp.minimum(t + 1, win).astype(F32)
        pooled = wsum / count - u
        yg = jnp.dot(pooled.astype(BF16), pw_ref[g].astype(BF16), preferred_element_type=F32)
        yp_ref[:, cols] = (yg * ps_ref[:, cols]).astype(BF16)


def _conv_pool(pf, conv_w, pool_w, pool_scale, *, tm=512):
    S = pf.shape[0]
    tm = min(tm, S)
    H = POOL_HALO
    hb = tm // H

    def cur(p):
        return pl.BlockSpec((tm, GROUP), lambda i, p=p: (i, p))

    def halo(p):
        return pl.BlockSpec((H, GROUP), lambda i, p=p: (jnp.maximum(i * hb - 1, 0), p))

    return pl.pallas_call(
        _conv_pool_kernel,
        grid=(S // tm,),
        in_specs=[cur(P_CB), cur(P_CC), cur(P_CU), cur(P_PU), halo(P_CC), halo(P_CU), halo(P_PU),
                  pl.BlockSpec((CONV_WIDTH, GROUP), lambda i: (0, 0)),
                  pl.BlockSpec(pool_w.shape, lambda i: (0, 0, 0)),
                  pl.BlockSpec((1, GROUP), lambda i: (0, 0))],
        out_specs=[pl.BlockSpec((tm, GROUP), lambda i: (i, 0)),
                   pl.BlockSpec((tm, GROUP), lambda i: (i, 0))],
        out_shape=[jax.ShapeDtypeStruct((S, GROUP), BF16), jax.ShapeDtypeStruct((S, GROUP), BF16)],
        scratch_shapes=[pltpu.VMEM((H + tm, GROUP), F32), pltpu.VMEM((H + tm, GROUP), F32)],
        compiler_params=_params("arbitrary"),
        name="conv_pool",
    )(pf, pf, pf, pf, pf, pf, pf, conv_w, pool_w, pool_scale)


def _log_sigmoid(x):
    return jnp.minimum(x, 0.0) - jnp.log(1.0 + jnp.exp(-jnp.abs(x)))


def _mlstm_kernel(q_ref, k_ref, v_ref, o_ref, g_ref, gt_ref, brow_ref, bcol_ref, gain_ref,
                  y_ref, c_ref, m_ref):
    L = MLSTM_CHUNK
    dh = HEAD_DIM

    @pl.when(pl.program_id(0) == 0)
    def _():
        c_ref[...] = jnp.zeros_like(c_ref)
        m_ref[...] = jnp.zeros_like(m_ref)

    G = g_ref[...] + brow_ref[...]
    GT = gt_ref[...] + bcol_ref[...]
    row = lax.broadcasted_iota(jnp.int32, (L, L), 0)
    col = lax.broadcasted_iota(jnp.int32, (L, L), 1)
    causal = col <= row
    tri = causal.astype(BF16)
    tri_t = (row <= col).astype(BF16)
    bc = sum(jnp.dot(tri, part, preferred_element_type=F32) for part in _split3(_log_sigmoid(G)))
    br = sum(jnp.dot(part, tri_t, preferred_element_type=F32) for part in _split3(_log_sigmoid(GT)))
    ones_col = (lax.broadcasted_iota(jnp.int32, (L, dh), 1) == 0).astype(BF16)

    for h in range(HEADS):
        cols = slice(h * dh, (h + 1) * dh)
        q = q_ref[:, cols]
        k = k_ref[:, cols]
        v = v_ref[:, cols]
        b_col = bc[:, HEADS + h:HEADS + h + 1]
        b_row = br[HEADS + h:HEADS + h + 1, :]
        ig_col = G[:, h:h + 1]
        ig_row = GT[h:h + 1, :]
        m_prev = m_ref[h:h + 1, 0:1]
        c_ext = c_ref[h]

        D = jnp.where(causal, b_col - b_row + ig_row, NEG_BIG)
        inter = b_col + m_prev
        m_t = jnp.maximum(inter, jnp.max(D, axis=1, keepdims=True))
        w_intra = jnp.exp(D - m_t)
        w_inter = jnp.exp(inter - m_t)
        qk = lax.dot_general(q, k, (((1,), (1,)), ((), ())), preferred_element_type=F32)
        s = qk * w_intra
        qc = jnp.dot(q, c_ext.astype(BF16), preferred_element_type=F32)
        num = jnp.dot(s.astype(BF16), v, preferred_element_type=F32) + w_inter * qc[:, :dh]
        den = jnp.sum(s, axis=1, keepdims=True) + w_inter * qc[:, dh:dh + 1]
        hh = num / jnp.maximum(jnp.abs(den), jnp.exp(-m_t))
        hh = hh * lax.rsqrt(jnp.mean(hh * hh, axis=1, keepdims=True) + NORM_EPS)
        o = o_ref[:, cols]
        y_ref[:, cols] = (hh * gain_ref[:, cols] / (1.0 + jnp.exp(-o))).astype(BF16)

        b_last = b_col[L - 1:L, :]
        g_col = b_last - b_col + ig_col
        m_new = jnp.maximum(b_last + m_prev, jnp.max(g_col, axis=0, keepdims=True))
        decay = jnp.exp(b_last + m_prev - m_new)
        wk = jnp.exp(g_col - m_new)
        kw_t = (wk * k.astype(F32)).T.astype(BF16)
        v_ext = jnp.concatenate([v, ones_col], axis=1)
        c_ref[h] = decay * c_ext + jnp.dot(kw_t, v_ext, preferred_element_type=F32)
        m_ref[h:h + 1, :] = jnp.broadcast_to(m_new, (1, LANES))


def _mlstm(pf, pb, gates, gates_t, bias_row, bias_col, head_gain):
    S = pf.shape[0]
    L = MLSTM_CHUNK
    return pl.pallas_call(
        _mlstm_kernel,
        grid=(S // L,),
        in_specs=[pl.BlockSpec((L, GROUP), lambda c: (c, P_MQ)),
                  pl.BlockSpec((L, GROUP), lambda c: (c, P_MK)),
                  pl.BlockSpec((L, GROUP), lambda c: (c, P_MV)),
                  pl.BlockSpec((L, GROUP), lambda c: (c, P_MO)),
                  pl.BlockSpec((L, LANES), lambda c: (c, 0)),
                  pl.BlockSpec((LANES, L), lambda c: (0, c)),
                  pl.BlockSpec((1, LANES), lambda c: (0, 0)),
                  pl.BlockSpec((LANES, LANES), lambda c: (0, 0)),
                  pl.BlockSpec((1, GROUP), lambda c: (0, 0))],
        out_specs=pl.BlockSpec((L, GROUP), lambda c: (c, 0)),
        out_shape=jax.ShapeDtypeStruct((S, GROUP), BF16),
        scratch_shapes=[pltpu.VMEM((HEADS, HEAD_DIM, 2 * HEAD_DIM), F32), pltpu.VMEM((8, LANES), F32)],
        compiler_params=_params("arbitrary"),
        name="mlstm",
    )(pb, pb, pb, pf, gates, gates_t, bias_row, bias_col, head_gain)


def _softplus(z):
    return jnp.maximum(z, 0.0) + jnp.log(1.0 + jnp.exp(-jnp.abs(z)))


def _sb_kernel(q_ref, k_ref, v_ref, u_ref, y_ref, acc_ref, carry_ref, *, tq, tk):
    i = pl.program_id(1)
    q = q_ref[...]
    acc_ref[...] = jnp.zeros_like(acc_ref)
    carry_ref[...] = jnp.zeros_like(carry_ref)
    q_pos = i * tq + lax.broadcasted_iota(jnp.int32, (tq, tk), 0)
    k_off = lax.broadcasted_iota(jnp.int32, (tq, tk), 1)

    def block(j, masked):
        rows = pl.ds(pl.multiple_of(j * tk, tk), tk)
        kb = k_ref[rows, :]
        vb = v_ref[rows, :]
        z = lax.dot_general(q, kb, (((1,), (1,)), ((), ())), preferred_element_type=F32)
        sp = _softplus(z)
        if masked:
            past = (j * tk + k_off) < q_pos
            sp = jnp.where(past, sp, 0.0)
        hi = sp.astype(BF16)
        lo = (sp - hi.astype(F32)).astype(BF16)
        R = (jnp.dot(hi, u_ref[...], preferred_element_type=F32)
             + jnp.dot(lo, u_ref[...], preferred_element_type=F32))
        carry = carry_ref[...]
        A = jnp.exp(z - R - carry)
        if masked:
            A = jnp.where(past, A, 0.0)
        acc_ref[...] += jnp.dot(A.astype(BF16), vb, preferred_element_type=F32)
        carry_ref[...] = carry + R[:, 0:1]

    n_diag = tq // tk
    j_hi = (i + 1) * n_diag
    for d in range(n_diag):
        block(j_hi - 1 - d, True)

    def body(jj, c):
        block(j_hi - n_diag - 1 - jj, False)
        return c

    lax.fori_loop(0, j_hi - n_diag, body, 0)
    y_ref[...] = acc_ref[...].astype(BF16)


def _sb_attention(pb, tri_u, *, tq=256, tk=256):
    S = pb.shape[0]
    tq = min(tq, S)
    tk = min(tk, tq)
    dh = HEAD_DIM
    qcol, kcol, vcol = P_SQ * HEADS, P_SK * HEADS, P_SV * HEADS
    return pl.pallas_call(
        functools.partial(_sb_kernel, tq=tq, tk=tk),
        grid=(HEADS, S // tq),
        in_specs=[pl.BlockSpec((tq, dh), lambda h, i: (i, qcol + h)),
                  pl.BlockSpec((S, dh), lambda h, i: (0, kcol + h)),
                  pl.BlockSpec((S, dh), lambda h, i: (0, vcol + h)),
                  pl.BlockSpec((tk, tk), lambda h, i: (0, 0))],
        out_specs=pl.BlockSpec((tq, dh), lambda h, i: (i, h)),
        out_shape=jax.ShapeDtypeStruct((S, GROUP), BF16),
        scratch_shapes=[pltpu.VMEM((tq, dh), F32), pltpu.VMEM((tq, 1), F32)],
        compiler_params=_params("arbitrary", "arbitrary"),
        name="sb_attention",
    )(pb, pb, pb, tri_u)


def _pack_w_in(w_in):
    G = GROUP
    gate0 = 7 * G
    after = gate0 + 2 * HEADS

    def panel(start):
        return w_in[:, start:start + G]

    main = jnp.concatenate(
        [panel(0), panel(G), panel(2 * G), panel(6 * G), panel(after),
         panel(3 * G), panel(4 * G), panel(5 * G),
         panel(after + G), panel(after + 2 * G), panel(after + 3 * G)], axis=1).astype(BF16)
    gate = jnp.pad(w_in[:, gate0:after], ((0, 0), (0, LANES - 2 * HEADS))).astype(BF16)
    return main, gate, gate.T


def _token_mix(h, x, w_in, w_out, conv_w, pool_w, pool_scale, i_bias, f_bias, head_gain,
               g_post, g_next, tri_u):
    w_main, w_gate, w_gate_t = _pack_w_in(w_in)
    pf, pb, gates, gates_t = _in_proj(h, w_main, w_gate, w_gate_t)
    y_conv, y_pool = _conv_pool(pf, conv_w, pool_w, pool_scale[None, :])
    bias = jnp.pad(jnp.concatenate([i_bias, f_bias]), (0, LANES - 2 * HEADS))
    bias_col = jnp.broadcast_to(bias[:, None], (LANES, LANES))
    y_mlstm = _mlstm(pf, pb, gates, gates_t, bias[None, :], bias_col, head_gain[None, :])
    y_sb = _sb_attention(pb, tri_u)
    return _proj_res([y_conv, y_mlstm, y_pool, y_sb], w_out, x, g_post, g_next, res_scale=1.0)


def kernel(x, w_in, w_out, conv_w, pool_w, pool_scale, mlstm_i_bias, mlstm_f_bias, mlstm_head_gain,
           ffn1_w_gate, ffn1_w_up, ffn1_w_down, ffn2_w_gate, ffn2_w_up, ffn2_w_down, norm_gains):
    B, S, D = x.shape
    depth = w_in.shape[0]
    outs = []
    tk = min(256, S)
    tri_u = (jnp.arange(tk)[:, None] >= jnp.arange(tk)[None, :]).astype(BF16)
    for b in range(B):
        xb = x[b]
        h = _first_norm(xb, norm_gains[0, 0][None, :])
        for l in range(depth):
            g = norm_gains[l]
            g_after = norm_gains[l + 1, 0] if l + 1 < depth else g[0]
            act = _ffn_up(h, ffn1_w_gate[l], ffn1_w_up[l])
            xb, h = _proj_res([act], ffn1_w_down[l], xb, g[1][None, :], g[2][None, :], res_scale=0.5)
            xb, h = _token_mix(h, xb, w_in[l], w_out[l], conv_w[l], pool_w[l], pool_scale[l],
                               mlstm_i_bias[l], mlstm_f_bias[l], mlstm_head_gain[l],
                               g[3][None, :], g[4][None, :], tri_u)
            act = _ffn_up(h, ffn2_w_gate[l], ffn2_w_up[l])
            xb, h = _proj_res([act], ffn2_w_down[l], xb, g[5][None, :], g_after[None, :], res_scale=0.5)
        outs.append(xb)
    return jnp.stack(outs, axis=0)
```

```python
import functools

import jax
import jax.numpy as jnp
from jax import lax
from jax.experimental import pallas as pl
from jax.experimental.pallas import tpu as pltpu

F32 = jnp.float32
BF16 = jnp.bfloat16

NORM_EPS = 1e-6
HEAD_DIM = 128
HEADS = 4
GROUP = HEADS * HEAD_DIM
CONV_WIDTH = 3
POOL_WINDOWS = (2, 4, 8, 16)
POOL_HALO = 16
MLSTM_CHUNK = 128
LANES = 128
NEG_BIG = -1e30
SB_DEAD_CARRY = 105.0

VMEM_LIMIT = 56 * 1024 * 1024

F32_PANELS = 5
BF16_PANELS = 6
P_CB, P_CC, P_CU, P_MO, P_PU = range(F32_PANELS)
P_MQ, P_MK, P_MV, P_SQ, P_SK, P_SV = range(BF16_PANELS)


def _params(*sem):
    return pltpu.CompilerParams(dimension_semantics=sem, vmem_limit_bytes=VMEM_LIMIT)


def _rms_rows(y, g):
    ms = jnp.mean(y * y, axis=-1, keepdims=True)
    return y * lax.rsqrt(ms + NORM_EPS) * g


def _split3(x):
    hi = x.astype(BF16)
    r1 = x - hi.astype(F32)
    mid = r1.astype(BF16)
    lo = (r1 - mid.astype(F32)).astype(BF16)
    return hi, mid, lo


def _norm_kernel(x_ref, g_ref, h_ref):
    h_ref[...] = _rms_rows(x_ref[...], g_ref[...]).astype(BF16)


def _first_norm(x, g):
    S, D = x.shape
    tm = 256
    return pl.pallas_call(
        _norm_kernel,
        grid=(S // tm,),
        in_specs=[pl.BlockSpec((tm, D), lambda i: (i, 0)),
                  pl.BlockSpec((1, D), lambda i: (0, 0))],
        out_specs=pl.BlockSpec((tm, D), lambda i: (i, 0)),
        out_shape=jax.ShapeDtypeStruct((S, D), BF16),
        compiler_params=_params("arbitrary"),
        name="first_norm",
    )(x, g)


def _ffn_up_kernel(h_ref, wg_ref, wu_ref, o_ref, wg_bf, wu_bf, *, row_chunk):
    @pl.when(pl.program_id(1) == 0)
    def _():
        wg_bf[...] = wg_ref[...].astype(BF16)
        wu_bf[...] = wu_ref[...].astype(BF16)

    def body(r, carry):
        rows = pl.ds(pl.multiple_of(r * row_chunk, row_chunk), row_chunk)
        h = h_ref[rows, :]
        g = jnp.dot(h, wg_bf[...], preferred_element_type=F32)
        u = jnp.dot(h, wu_bf[...], preferred_element_type=F32)
        o_ref[rows, :] = (g / (1.0 + jnp.exp(-g)) * u).astype(BF16)
        return carry

    lax.fori_loop(0, h_ref.shape[0] // row_chunk, body, 0)


def _ffn_up(h, wg, wu, layer, *, tm=1024, tn=512, row_chunk=256):
    S, D = h.shape
    N = wg.shape[2]
    tm = min(tm, S)
    return pl.pallas_call(
        functools.partial(_ffn_up_kernel, row_chunk=min(row_chunk, tm)),
        grid=(N // tn, S // tm),
        in_specs=[pl.BlockSpec((tm, D), lambda j, i: (i, 0)),
                  pl.BlockSpec((None, D, tn), lambda j, i: (layer, 0, j)),
                  pl.BlockSpec((None, D, tn), lambda j, i: (layer, 0, j))],
        out_specs=pl.BlockSpec((tm, tn), lambda j, i: (i, j)),
        out_shape=jax.ShapeDtypeStruct((S, N), BF16),
        scratch_shapes=[pltpu.VMEM((D, tn), BF16), pltpu.VMEM((D, tn), BF16)],
        compiler_params=_params("arbitrary", "arbitrary"),
        name="ffn_up",
    )(h, wg, wu)


def _proj_res_kernel(*refs, n_lhs, nk, res_scale, row_chunk, ep_chunk):
    a_refs = refs[:n_lhs]
    w_ref, x_ref, gp_ref, gn_ref, xo_ref, ho_ref = refs[n_lhs:]
    k = pl.program_id(1)
    tm = xo_ref.shape[0]
    kw = a_refs[0].shape[1]

    def accumulate(first):
        def body(r, carry):
            rows = pl.ds(pl.multiple_of(r * row_chunk, row_chunk), row_chunk)
            part = jnp.dot(a_refs[0][rows, :], w_ref[0:kw, :], preferred_element_type=F32)
            for g in range(1, n_lhs):
                part = part + jnp.dot(a_refs[g][rows, :], w_ref[g * kw:(g + 1) * kw, :],
                                      preferred_element_type=F32)
            if first:
                xo_ref[rows, :] = part
            else:
                xo_ref[rows, :] += part
            return carry
        lax.fori_loop(0, tm // row_chunk, body, 0)

    def epilogue():
        def body(r, carry):
            rows = pl.ds(pl.multiple_of(r * ep_chunk, ep_chunk), ep_chunk)
            xn = x_ref[rows, :] + res_scale * _rms_rows(xo_ref[rows, :], gp_ref[...])
            xo_ref[rows, :] = xn
            ho_ref[rows, :] = _rms_rows(xn, gn_ref[...]).astype(BF16)
            return carry
        lax.fori_loop(0, tm // ep_chunk, body, 0)

    if nk == 1:
        accumulate(True)
        epilogue()
    else:
        pl.when(k == 0)(lambda: accumulate(True))
        pl.when(k > 0)(lambda: accumulate(False))
        pl.when(k == nk - 1)(epilogue)


def _proj_res(lhs, w, layer, x, g_post, g_next, *, res_scale, tm=512, tk=1408, row_chunk=256, ep_chunk=16):
    S, D = x.shape
    tm = min(tm, S)
    n_lhs = len(lhs)
    kw = lhs[0].shape[1]
    if n_lhs > 1:
        tk = n_lhs * kw
    nk = w.shape[1] // tk
    return pl.pallas_call(
        functools.partial(_proj_res_kernel, n_lhs=n_lhs, nk=nk, res_scale=res_scale,
                          row_chunk=min(row_chunk, tm), ep_chunk=ep_chunk),
        grid=(S // tm, nk),
        in_specs=[pl.BlockSpec((tm, tk // n_lhs), lambda i, k: (i, k)) for _ in lhs] + [
            pl.BlockSpec((None, tk, D), lambda i, k: (layer, k, 0)),
            pl.BlockSpec((tm, D), lambda i, k: (i, 0), pipeline_mode=pl.Buffered(1)),
            pl.BlockSpec((1, D), lambda i, k: (0, 0)),
            pl.BlockSpec((1, D), lambda i, k: (0, 0))],
        out_specs=[pl.BlockSpec((tm, D), lambda i, k: (i, 0)),
                   pl.BlockSpec((tm, D), lambda i, k: (i, 0))],
        out_shape=[jax.ShapeDtypeStruct((S, D), F32), jax.ShapeDtypeStruct((S, D), BF16)],
        compiler_params=_params("arbitrary", "arbitrary"),
        name="proj_res",
    )(*lhs, w, x, g_post, g_next)


def _in_proj_kernel(h_ref, w_ref, wg_ref, wgt_ref, of_ref, ob_ref, g_ref, gt_ref, *, scale):
    h = h_ref[...]
    for p in range(F32_PANELS):
        cols = slice(p * GROUP, (p + 1) * GROUP)
        of_ref[:, cols] = jnp.dot(h, w_ref[:, cols], preferred_element_type=F32)
    for p in range(BF16_PANELS):
        cols = slice(p * GROUP, (p + 1) * GROUP)
        wcols = slice((F32_PANELS + p) * GROUP, (F32_PANELS + p + 1) * GROUP)
        r = jnp.dot(h, w_ref[:, wcols], preferred_element_type=F32)
        if p in (P_MK, P_SQ):
            r = r * scale
        ob_ref[:, cols] = r.astype(BF16)
    g_ref[...] = jnp.dot(h, wg_ref[...], preferred_element_type=F32)
    gt_ref[...] = lax.dot_general(wgt_ref[...], h, (((1,), (1,)), ((), ())), preferred_element_type=F32)


def _in_proj(h, w_main, w_gate, w_gate_t, layer, *, tm=512):
    S, D = h.shape
    tm = min(tm, S)
    nf, nb = F32_PANELS * GROUP, BF16_PANELS * GROUP
    return pl.pallas_call(
        functools.partial(_in_proj_kernel, scale=HEAD_DIM ** -0.5),
        grid=(S // tm,),
        in_specs=[pl.BlockSpec((tm, D), lambda i: (i, 0)),
                  pl.BlockSpec((None, D, nf + nb), lambda i: (layer, 0, 0), pipeline_mode=pl.Buffered(1)),
                  pl.BlockSpec((None, D, LANES), lambda i: (layer, 0, 0)),
                  pl.BlockSpec((None, LANES, D), lambda i: (layer, 0, 0))],
        out_specs=[pl.BlockSpec((tm, nf), lambda i: (i, 0)),
                   pl.BlockSpec((tm, nb), lambda i: (i, 0)),
                   pl.BlockSpec((tm, LANES), lambda i: (i, 0)),
                   pl.BlockSpec((LANES, tm), lambda i: (0, i))],
        out_shape=[jax.ShapeDtypeStruct((S, nf), F32),
                   jax.ShapeDtypeStruct((S, nb), BF16),
                   jax.ShapeDtypeStruct((S, LANES), F32),
                   jax.ShapeDtypeStruct((LANES, S), F32)],
        compiler_params=_params("arbitrary"),
        name="in_proj",
    )(h, w_main, w_gate, w_gate_t)


def _conv_pool_kernel(cb_ref, cc_ref, cu_ref, pu_ref, cch_ref, cuh_ref, puh_ref,
                      cw_ref, pw_ref, ps_ref, yc_ref, yp_ref, zbuf, pbuf):
    i = pl.program_id(0)
    tm = cb_ref.shape[0]
    keep = (i > 0).astype(F32)
    H = POOL_HALO

    zbuf[0:H, :] = cch_ref[...] * cuh_ref[...] * keep
    zbuf[H:H + tm, :] = cc_ref[...] * cu_ref[...]
    y = cw_ref[0:1, :] * zbuf[H - 2:H - 2 + tm, :]
    y = y + cw_ref[1:2, :] * zbuf[H - 1:H - 1 + tm, :]
    y = y + cw_ref[2:3, :] * zbuf[H:H + tm, :]
    yc_ref[...] = (cb_ref[...] * y).astype(BF16)

    pbuf[0:H, :] = puh_ref[...] * keep
    pbuf[H:H + tm, :] = pu_ref[...]
    t = i * tm + lax.broadcasted_iota(jnp.int32, (tm, 1), 0)
    for g, win in enumerate(POOL_WINDOWS):
        cols = slice(g * LANES, (g + 1) * LANES)
        u = pbuf[H:H + tm, cols]
        wsum = u
        for d in range(1, win):
            wsum = wsum + pbuf[H - d:H - d + tm, cols]
        count = jnp.minimum(t + 1, win).astype(F32)
        pooled = wsum / count - u
        yg = jnp.dot(pooled.astype(BF16), pw_ref[g].astype(BF16), preferred_element_type=F32)
        yp_ref[:, cols] = (yg * ps_ref[:, cols]).astype(BF16)


def _conv_pool(pf, conv_w, pool_w, pool_scale, *, tm=512):
    S = pf.shape[0]
    tm = min(tm, S)
    H = POOL_HALO
    hb = tm // H

    def cur(p):
        return pl.BlockSpec((tm, GROUP), lambda i, p=p: (i, p))

    def halo(p):
        return pl.BlockSpec((H, GROUP), lambda i, p=p: (jnp.maximum(i * hb - 1, 0), p))

    return pl.pallas_call(
        _conv_pool_kernel,
        grid=(S // tm,),
        in_specs=[cur(P_CB), cur(P_CC), cur(P_CU), cur(P_PU), halo(P_CC), halo(P_CU), halo(P_PU),
                  pl.BlockSpec((CONV_WIDTH, GROUP), lambda i: (0, 0)),
                  pl.BlockSpec(pool_w.shape, lambda i: (0, 0, 0)),
                  pl.BlockSpec((1, GROUP), lambda i: (0, 0))],
        out_specs=[pl.BlockSpec((tm, GROUP), lambda i: (i, 0)),
                   pl.BlockSpec((tm, GROUP), lambda i: (i, 0))],
        out_shape=[jax.ShapeDtypeStruct((S, GROUP), BF16), jax.ShapeDtypeStruct((S, GROUP), BF16)],
        scratch_shapes=[pltpu.VMEM((H + tm, GROUP), F32), pltpu.VMEM((H + tm, GROUP), F32)],
        compiler_params=_params("arbitrary"),
        name="conv_pool",
    )(pf, pf, pf, pf, pf, pf, pf, conv_w, pool_w, pool_scale)


def _log_sigmoid(x):
    return jnp.minimum(x, 0.0) - jnp.log(1.0 + jnp.exp(-jnp.abs(x)))


def _mlstm_kernel(q_ref, k_ref, v_ref, o_ref, g_ref, gt_ref, brow_ref, bcol_ref, gain_ref,
                  y_ref, c_ref, m_ref):
    L = MLSTM_CHUNK
    dh = HEAD_DIM

    @pl.when(pl.program_id(0) == 0)
    def _():
        c_ref[...] = jnp.zeros_like(c_ref)
        m_ref[...] = jnp.zeros_like(m_ref)

    G = g_ref[...] + brow_ref[...]
    GT = gt_ref[...] + bcol_ref[...]
    row = lax.broadcasted_iota(jnp.int32, (L, L), 0)
    col = lax.broadcasted_iota(jnp.int32, (L, L), 1)
    causal = col <= row
    tri = causal.astype(BF16)
    tri_t = (row <= col).astype(BF16)
    bc = sum(jnp.dot(tri, part, preferred_element_type=F32) for part in _split3(_log_sigmoid(G)))
    br = sum(jnp.dot(part, tri_t, preferred_element_type=F32) for part in _split3(_log_sigmoid(GT)))
    ones_col = (lax.broadcasted_iota(jnp.int32, (L, dh), 1) == 0).astype(BF16)

    for h in range(HEADS):
        cols = slice(h * dh, (h + 1) * dh)
        q = q_ref[:, cols]
        k = k_ref[:, cols]
        v = v_ref[:, cols]
        b_col = bc[:, HEADS + h:HEADS + h + 1]
        b_row = br[HEADS + h:HEADS + h + 1, :]
        ig_col = G[:, h:h + 1]
        ig_row = GT[h:h + 1, :]
        m_prev = m_ref[h:h + 1, 0:1]
        c_ext = c_ref[h]

        D = jnp.where(causal, b_col - b_row + ig_row, NEG_BIG)
        inter = b_col + m_prev
        m_t = jnp.maximum(inter, jnp.max(D, axis=1, keepdims=True))
        w_intra = jnp.exp(D - m_t)
        w_inter = jnp.exp(inter - m_t)
        qk = lax.dot_general(q, k, (((1,), (1,)), ((), ())), preferred_element_type=F32)
        s = qk * w_intra
        qc = jnp.dot(q, c_ext.astype(BF16), preferred_element_type=F32)
        num = jnp.dot(s.astype(BF16), v, preferred_element_type=F32) + w_inter * qc[:, :dh]
        den = jnp.sum(s, axis=1, keepdims=True) + w_inter * qc[:, dh:dh + 1]
        hh = num / jnp.maximum(jnp.abs(den), jnp.exp(-m_t))
        hh = hh * lax.rsqrt(jnp.mean(hh * hh, axis=1, keepdims=True) + NORM_EPS)
        o = o_ref[:, cols]
        y_ref[:, cols] = (hh * gain_ref[:, cols] / (1.0 + jnp.exp(-o))).astype(BF16)

        b_last = b_col[L - 1:L, :]
        g_col = b_last - b_col + ig_col
        m_new = jnp.maximum(b_last + m_prev, jnp.max(g_col, axis=0, keepdims=True))
        decay = jnp.exp(b_last + m_prev - m_new)
        wk = jnp.exp(g_col - m_new)
        kw_t = (wk * k.astype(F32)).T.astype(BF16)
        v_ext = jnp.concatenate([v, ones_col], axis=1)
        c_ref[h] = decay * c_ext + jnp.dot(kw_t, v_ext, preferred_element_type=F32)
        m_ref[h:h + 1, :] = jnp.broadcast_to(m_new, (1, LANES))


def _mlstm(pf, pb, gates, gates_t, bias_row, bias_col, head_gain):
    S = pf.shape[0]
    L = MLSTM_CHUNK
    return pl.pallas_call(
        _mlstm_kernel,
        grid=(S // L,),
        in_specs=[pl.BlockSpec((L, GROUP), lambda c: (c, P_MQ)),
                  pl.BlockSpec((L, GROUP), lambda c: (c, P_MK)),
                  pl.BlockSpec((L, GROUP), lambda c: (c, P_MV)),
                  pl.BlockSpec((L, GROUP), lambda c: (c, P_MO)),
                  pl.BlockSpec((L, LANES), lambda c: (c, 0)),
                  pl.BlockSpec((LANES, L), lambda c: (0, c)),
                  pl.BlockSpec((1, LANES), lambda c: (0, 0)),
                  pl.BlockSpec((LANES, LANES), lambda c: (0, 0)),
                  pl.BlockSpec((1, GROUP), lambda c: (0, 0))],
        out_specs=pl.BlockSpec((L, GROUP), lambda c: (c, 0)),
        out_shape=jax.ShapeDtypeStruct((S, GROUP), BF16),
        scratch_shapes=[pltpu.VMEM((HEADS, HEAD_DIM, 2 * HEAD_DIM), F32), pltpu.VMEM((8, LANES), F32)],
        compiler_params=_params("arbitrary"),
        name="mlstm",
    )(pb, pb, pb, pf, gates, gates_t, bias_row, bias_col, head_gain)


def _softplus(z):
    return jnp.maximum(z, 0.0) + jnp.log(1.0 + jnp.exp(-jnp.abs(z)))


def _sb_kernel(q_ref, k_ref, v_ref, u_ref, y_ref, acc_ref, carry_ref, *, tq, tk):
    i = pl.program_id(1)
    q = q_ref[...]
    acc_ref[...] = jnp.zeros_like(acc_ref)
    carry_ref[...] = jnp.zeros_like(carry_ref)
    q_pos = i * tq + lax.broadcasted_iota(jnp.int32, (tq, tk), 0)
    k_off = lax.broadcasted_iota(jnp.int32, (tq, tk), 1)

    def block(j, masked):
        rows = pl.ds(pl.multiple_of(j * tk, tk), tk)
        kb = k_ref[rows, :]
        vb = v_ref[rows, :]
        z = lax.dot_general(q, kb, (((1,), (1,)), ((), ())), preferred_element_type=F32)
        sp = _softplus(z)
        if masked:
            past = (j * tk + k_off) < q_pos
            sp = jnp.where(past, sp, 0.0)
        hi = sp.astype(BF16)
        lo = (sp - hi.astype(F32)).astype(BF16)
        R = (jnp.dot(hi, u_ref[...], preferred_element_type=F32)
             + jnp.dot(lo, u_ref[...], preferred_element_type=F32))
        carry = carry_ref[...]
        A = jnp.exp(z - R - carry)
        if masked:
            A = jnp.where(past, A, 0.0)
        acc_ref[...] += jnp.dot(A.astype(BF16), vb, preferred_element_type=F32)
        carry_ref[...] = carry + R[:, 0:1]

    n_diag = tq // tk
    j_hi = (i + 1) * n_diag
    for d in range(n_diag):
        block(j_hi - 1 - d, True)

    def cond(state):
        jj, live = state
        return jnp.logical_and(jj < j_hi - n_diag, live)

    def body(state):
        jj, _ = state
        block(j_hi - n_diag - 1 - jj, False)
        return jj + 1, jnp.min(carry_ref[...]) < SB_DEAD_CARRY

    lax.while_loop(cond, body, (jnp.int32(0), jnp.min(carry_ref[...]) < SB_DEAD_CARRY))
    y_ref[...] = acc_ref[...].astype(BF16)


def _sb_attention(pb, tri_u, *, tq=256, tk=256):
    S = pb.shape[0]
    tq = min(tq, S)
    tk = min(tk, tq)
    dh = HEAD_DIM
    qcol, kcol, vcol = P_SQ * HEADS, P_SK * HEADS, P_SV * HEADS
    return pl.pallas_call(
        functools.partial(_sb_kernel, tq=tq, tk=tk),
        grid=(HEADS, S // tq),
        in_specs=[pl.BlockSpec((tq, dh), lambda h, i: (i, qcol + h)),
                  pl.BlockSpec((S, dh), lambda h, i: (0, kcol + h)),
                  pl.BlockSpec((S, dh), lambda h, i: (0, vcol + h)),
                  pl.BlockSpec((tk, tk), lambda h, i: (0, 0))],
        out_specs=pl.BlockSpec((tq, dh), lambda h, i: (i, h)),
        out_shape=jax.ShapeDtypeStruct((S, GROUP), BF16),
        scratch_shapes=[pltpu.VMEM((tq, dh), F32), pltpu.VMEM((tq, 1), F32)],
        compiler_params=_params("arbitrary", "arbitrary"),
        name="sb_attention",
    )(pb, pb, pb, tri_u)


def _pack_w_in(w_in):
    G = GROUP
    gate0 = 7 * G
    after = gate0 + 2 * HEADS

    def panel(start):
        return w_in[:, :, start:start + G]

    main = jnp.concatenate(
        [panel(0), panel(G), panel(2 * G), panel(6 * G), panel(after),
         panel(3 * G), panel(4 * G), panel(5 * G),
         panel(after + G), panel(after + 2 * G), panel(after + 3 * G)], axis=2).astype(BF16)
    gate = jnp.pad(w_in[:, :, gate0:after], ((0, 0), (0, 0), (0, LANES - 2 * HEADS))).astype(BF16)
    return main, gate, jnp.swapaxes(gate, 1, 2)


def _token_mix(h, x, layer, w_main, w_gate, w_gate_t, w_out, conv_w, pool_w, pool_scale, i_bias, f_bias,
               head_gain, g_post, g_next, tri_u):
    pf, pb, gates, gates_t = _in_proj(h, w_main, w_gate, w_gate_t, layer)
    y_conv, y_pool = _conv_pool(pf, conv_w, pool_w, pool_scale[None, :])
    bias = jnp.pad(jnp.concatenate([i_bias, f_bias]), (0, LANES - 2 * HEADS))
    bias_col = jnp.broadcast_to(bias[:, None], (LANES, LANES))
    y_mlstm = _mlstm(pf, pb, gates, gates_t, bias[None, :], bias_col, head_gain[None, :])
    y_sb = _sb_attention(pb, tri_u)
    return _proj_res([y_conv, y_mlstm, y_pool, y_sb], w_out, layer, x, g_post, g_next, res_scale=1.0)


def kernel(x, w_in, w_out, conv_w, pool_w, pool_scale, mlstm_i_bias, mlstm_f_bias, mlstm_head_gain,
           ffn1_w_gate, ffn1_w_up, ffn1_w_down, ffn2_w_gate, ffn2_w_up, ffn2_w_down, norm_gains):
    B, S, D = x.shape
    depth = w_in.shape[0]
    outs = []
    tk = min(256, S)
    tri_u = (jnp.arange(tk)[:, None] >= jnp.arange(tk)[None, :]).astype(BF16)
    w_main, w_gate, w_gate_t = _pack_w_in(w_in)
    w_out_bf = w_out.astype(BF16)
    wd1_bf = ffn1_w_down.astype(BF16)
    wd2_bf = ffn2_w_down.astype(BF16)
    for b in range(B):
        xb = x[b]
        h = _first_norm(xb, norm_gains[0, 0][None, :])
        for l in range(depth):
            g = norm_gains[l]
            g_after = norm_gains[l + 1, 0] if l + 1 < depth else g[0]
            act = _ffn_up(h, ffn1_w_gate, ffn1_w_up, l)
            xb, h = _proj_res([act], wd1_bf, l, xb, g[1][None, :], g[2][None, :], res_scale=0.5)
            xb, h = _token_mix(h, xb, l, w_main, w_gate, w_gate_t, w_out_bf, conv_w[l], pool_w[l],
                               pool_scale[l], mlstm_i_bias[l], mlstm_f_bias[l], mlstm_head_gain[l],
                               g[3][None, :], g[4][None, :], tri_u)
            act = _ffn_up(h, ffn2_w_gate, ffn2_w_up, l)
            xb, h = _proj_res([act], wd2_bf, l, xb, g[5][None, :], g_after[None, :], res_scale=0.5)
        outs.append(xb)
    return jnp.stack(outs, axis=0)
```

```python
import functools

import jax
import jax.numpy as jnp
from jax import lax
from jax.experimental import pallas as pl
from jax.experimental.pallas import tpu as pltpu

F32 = jnp.float32
BF16 = jnp.bfloat16

NORM_EPS = 1e-6
HEAD_DIM = 128
HEADS = 4
GROUP = HEADS * HEAD_DIM
CONV_WIDTH = 3
POOL_WINDOWS = (2, 4, 8, 16)
POOL_HALO = 16
MLSTM_CHUNK = 128
LANES = 128
NEG_BIG = -1e30
SB_DEAD_CARRY = 105.0

VMEM_LIMIT = 56 * 1024 * 1024

F32_PANELS = 5
BF16_PANELS = 6
P_CB, P_CC, P_CU, P_MO, P_PU = range(F32_PANELS)
P_MQ, P_MK, P_MV, P_SQ, P_SK, P_SV = range(BF16_PANELS)


def _params(*sem):
    return pltpu.CompilerParams(dimension_semantics=sem, vmem_limit_bytes=VMEM_LIMIT)


def _rms_rows(y, g):
    ms = jnp.mean(y * y, axis=-1, keepdims=True)
    return y * lax.rsqrt(ms + NORM_EPS) * g


def _split3(x):
    hi = x.astype(BF16)
    r1 = x - hi.astype(F32)
    mid = r1.astype(BF16)
    lo = (r1 - mid.astype(F32)).astype(BF16)
    return hi, mid, lo


def _norm_kernel(x_ref, g_ref, h_ref):
    h_ref[...] = _rms_rows(x_ref[...], g_ref[...]).astype(BF16)


def _first_norm(x, g):
    S, D = x.shape
    tm = 256
    return pl.pallas_call(
        _norm_kernel,
        grid=(S // tm,),
        in_specs=[pl.BlockSpec((tm, D), lambda i: (i, 0)),
                  pl.BlockSpec((1, D), lambda i: (0, 0))],
        out_specs=pl.BlockSpec((tm, D), lambda i: (i, 0)),
        out_shape=jax.ShapeDtypeStruct((S, D), BF16),
        compiler_params=_params("arbitrary"),
        name="first_norm",
    )(x, g)


def _ffn_up_kernel(h_ref, wg_ref, wu_ref, o_ref, wg_bf, wu_bf, *, row_chunk):
    @pl.when(pl.program_id(1) == 0)
    def _():
        wg_bf[...] = wg_ref[...].astype(BF16)
        wu_bf[...] = wu_ref[...].astype(BF16)

    def body(r, carry):
        rows = pl.ds(pl.multiple_of(r * row_chunk, row_chunk), row_chunk)
        h = h_ref[rows, :]
        g = jnp.dot(h, wg_bf[...], preferred_element_type=F32)
        u = jnp.dot(h, wu_bf[...], preferred_element_type=F32)
        o_ref[rows, :] = (g / (1.0 + jnp.exp(-g)) * u).astype(BF16)
        return carry

    lax.fori_loop(0, h_ref.shape[0] // row_chunk, body, 0)


def _ffn_up(h, wg, wu, layer, *, tm=1024, tn=512, row_chunk=512):
    S, D = h.shape
    N = wg.shape[2]
    tm = min(tm, S)
    return pl.pallas_call(
        functools.partial(_ffn_up_kernel, row_chunk=min(row_chunk, tm)),
        grid=(N // tn, S // tm),
        in_specs=[pl.BlockSpec((tm, D), lambda j, i: (i, 0)),
                  pl.BlockSpec((None, D, tn), lambda j, i: (layer, 0, j)),
                  pl.BlockSpec((None, D, tn), lambda j, i: (layer, 0, j))],
        out_specs=pl.BlockSpec((tm, tn), lambda j, i: (i, j)),
        out_shape=jax.ShapeDtypeStruct((S, N), BF16),
        scratch_shapes=[pltpu.VMEM((D, tn), BF16), pltpu.VMEM((D, tn), BF16)],
        compiler_params=_params("arbitrary", "arbitrary"),
        name="ffn_up",
    )(h, wg, wu)


def _proj_res_kernel(*refs, n_lhs, nk, res_scale, row_chunk, ep_chunk):
    a_refs = refs[:n_lhs]
    w_ref, x_ref, gp_ref, gn_ref, xo_ref, ho_ref = refs[n_lhs:]
    k = pl.program_id(1)
    tm = xo_ref.shape[0]
    kw = a_refs[0].shape[1]

    def accumulate(first):
        def body(r, carry):
            rows = pl.ds(pl.multiple_of(r * row_chunk, row_chunk), row_chunk)
            part = jnp.dot(a_refs[0][rows, :], w_ref[0:kw, :], preferred_element_type=F32)
            for g in range(1, n_lhs):
                part = part + jnp.dot(a_refs[g][rows, :], w_ref[g * kw:(g + 1) * kw, :],
                                      preferred_element_type=F32)
            if first:
                xo_ref[rows, :] = part
            else:
                xo_ref[rows, :] += part
            return carry
        lax.fori_loop(0, tm // row_chunk, body, 0)

    def epilogue():
        def body(r, carry):
            rows = pl.ds(pl.multiple_of(r * ep_chunk, ep_chunk), ep_chunk)
            xn = x_ref[rows, :] + res_scale * _rms_rows(xo_ref[rows, :], gp_ref[...])
            xo_ref[rows, :] = xn
            ho_ref[rows, :] = _rms_rows(xn, gn_ref[...]).astype(BF16)
            return carry
        lax.fori_loop(0, tm // ep_chunk, body, 0, unroll=2)

    if nk == 1:
        accumulate(True)
        epilogue()
    else:
        pl.when(k == 0)(lambda: accumulate(True))
        pl.when(k > 0)(lambda: accumulate(False))
        pl.when(k == nk - 1)(epilogue)


def _proj_res(lhs, w, layer, x, g_post, g_next, *, res_scale, tm=512, tk=1408, row_chunk=512, ep_chunk=64):
    S, D = x.shape
    tm = min(tm, S)
    n_lhs = len(lhs)
    kw = lhs[0].shape[1]
    if n_lhs > 1:
        tk = n_lhs * kw
    nk = w.shape[1] // tk
    return pl.pallas_call(
        functools.partial(_proj_res_kernel, n_lhs=n_lhs, nk=nk, res_scale=res_scale,
                          row_chunk=min(row_chunk, tm), ep_chunk=ep_chunk),
        grid=(S // tm, nk),
        in_specs=[pl.BlockSpec((tm, tk // n_lhs), lambda i, k: (i, k)) for _ in lhs] + [
            pl.BlockSpec((None, tk, D), lambda i, k: (layer, k, 0)),
            pl.BlockSpec((tm, D), lambda i, k: (i, 0), pipeline_mode=pl.Buffered(1)),
            pl.BlockSpec((1, D), lambda i, k: (0, 0)),
            pl.BlockSpec((1, D), lambda i, k: (0, 0))],
        out_specs=[pl.BlockSpec((tm, D), lambda i, k: (i, 0)),
                   pl.BlockSpec((tm, D), lambda i, k: (i, 0))],
        out_shape=[jax.ShapeDtypeStruct((S, D), F32), jax.ShapeDtypeStruct((S, D), BF16)],
        compiler_params=_params("arbitrary", "arbitrary"),
        name="proj_res",
    )(*lhs, w, x, g_post, g_next)


def _in_proj_kernel(h_ref, w_ref, wg_ref, wgt_ref, of_ref, ob_ref, g_ref, gt_ref, *, scale):
    h = h_ref[...]
    for p in range(F32_PANELS):
        cols = slice(p * GROUP, (p + 1) * GROUP)
        of_ref[:, cols] = jnp.dot(h, w_ref[:, cols], preferred_element_type=F32)
    for p in range(BF16_PANELS):
        cols = slice(p * GROUP, (p + 1) * GROUP)
        wcols = slice((F32_PANELS + p) * GROUP, (F32_PANELS + p + 1) * GROUP)
        r = jnp.dot(h, w_ref[:, wcols], preferred_element_type=F32)
        if p in (P_MK, P_SQ):
            r = r * scale
        ob_ref[:, cols] = r.astype(BF16)
    g_ref[...] = jnp.dot(h, wg_ref[...], preferred_element_type=F32)
    gt_ref[...] = lax.dot_general(wgt_ref[...], h, (((1,), (1,)), ((), ())), preferred_element_type=F32)


def _in_proj(h, w_main, w_gate, w_gate_t, layer, *, tm=512):
    S, D = h.shape
    tm = min(tm, S)
    nf, nb = F32_PANELS * GROUP, BF16_PANELS * GROUP
    return pl.pallas_call(
        functools.partial(_in_proj_kernel, scale=HEAD_DIM ** -0.5),
        grid=(S // tm,),
        in_specs=[pl.BlockSpec((tm, D), lambda i: (i, 0)),
                  pl.BlockSpec((None, D, nf + nb), lambda i: (layer, 0, 0), pipeline_mode=pl.Buffered(1)),
                  pl.BlockSpec((None, D, LANES), lambda i: (layer, 0, 0)),
                  pl.BlockSpec((None, LANES, D), lambda i: (layer, 0, 0))],
        out_specs=[pl.BlockSpec((tm, nf), lambda i: (i, 0)),
                   pl.BlockSpec((tm, nb), lambda i: (i, 0)),
                   pl.BlockSpec((tm, LANES), lambda i: (i, 0)),
                   pl.BlockSpec((LANES, tm), lambda i: (0, i))],
        out_shape=[jax.ShapeDtypeStruct((S, nf), F32),
                   jax.ShapeDtypeStruct((S, nb), BF16),
                   jax.ShapeDtypeStruct((S, LANES), F32),
                   jax.ShapeDtypeStruct((LANES, S), F32)],
        compiler_params=_params("arbitrary"),
        name="in_proj",
    )(h, w_main, w_gate, w_gate_t)


def _conv_pool_kernel(cb_ref, cc_ref, cu_ref, pu_ref, cch_ref, cuh_ref, puh_ref,
                      cw_ref, pw_ref, ps_ref, yc_ref, yp_ref, zbuf, pbuf):
    i = pl.program_id(0)
    tm = cb_ref.shape[0]
    keep = (i > 0).astype(F32)
    H = POOL_HALO

    zbuf[0:H, :] = cch_ref[...] * cuh_ref[...] * keep
    zbuf[H:H + tm, :] = cc_ref[...] * cu_ref[...]
    y = cw_ref[0:1, :] * zbuf[H - 2:H - 2 + tm, :]
    y = y + cw_ref[1:2, :] * zbuf[H - 1:H - 1 + tm, :]
    y = y + cw_ref[2:3, :] * zbuf[H:H + tm, :]
    yc_ref[...] = (cb_ref[...] * y).astype(BF16)

    pbuf[0:H, :] = puh_ref[...] * keep
    pbuf[H:H + tm, :] = pu_ref[...]
    t = i * tm + lax.broadcasted_iota(jnp.int32, (tm, 1), 0)
    for g, win in enumerate(POOL_WINDOWS):
        cols = slice(g * LANES, (g + 1) * LANES)
        u = pbuf[H:H + tm, cols]
        wsum = u
        for d in range(1, win):
            wsum = wsum + pbuf[H - d:H - d + tm, cols]
        count = jnp.minimum(t + 1, win).astype(F32)
        pooled = wsum / count - u
        yg = jnp.dot(pooled.astype(BF16), pw_ref[g].astype(BF16), preferred_element_type=F32)
        yp_ref[:, cols] = (yg * ps_ref[:, cols]).astype(BF16)


def _conv_pool(pf, conv_w, pool_w, pool_scale, *, tm=512):
    S = pf.shape[0]
    tm = min(tm, S)
    H = POOL_HALO
    hb = tm // H

    def cur(p):
        return pl.BlockSpec((tm, GROUP), lambda i, p=p: (i, p))

    def halo(p):
        return pl.BlockSpec((H, GROUP), lambda i, p=p: (jnp.maximum(i * hb - 1, 0), p))

    return pl.pallas_call(
        _conv_pool_kernel,
        grid=(S // tm,),
        in_specs=[cur(P_CB), cur(P_CC), cur(P_CU), cur(P_PU), halo(P_CC), halo(P_CU), halo(P_PU),
                  pl.BlockSpec((CONV_WIDTH, GROUP), lambda i: (0, 0)),
                  pl.BlockSpec(pool_w.shape, lambda i: (0, 0, 0)),
                  pl.BlockSpec((1, GROUP), lambda i: (0, 0))],
        out_specs=[pl.BlockSpec((tm, GROUP), lambda i: (i, 0)),
                   pl.BlockSpec((tm, GROUP), lambda i: (i, 0))],
        out_shape=[jax.ShapeDtypeStruct((S, GROUP), BF16), jax.ShapeDtypeStruct((S, GROUP), BF16)],
        scratch_shapes=[pltpu.VMEM((H + tm, GROUP), F32), pltpu.VMEM((H + tm, GROUP), F32)],
        compiler_params=_params("arbitrary"),
        name="conv_pool",
    )(pf, pf, pf, pf, pf, pf, pf, conv_w, pool_w, pool_scale)


def _log_sigmoid(x):
    return jnp.minimum(x, 0.0) - jnp.log(1.0 + jnp.exp(-jnp.abs(x)))


def _mlstm_kernel(q_ref, k_ref, v_ref, o_ref, g_ref, gt_ref, brow_ref, bcol_ref, gain_ref,
                  y_ref, c_ref, m_ref):
    L = MLSTM_CHUNK
    dh = HEAD_DIM

    @pl.when(pl.program_id(0) == 0)
    def _():
        c_ref[...] = jnp.zeros_like(c_ref)
        m_ref[...] = jnp.zeros_like(m_ref)

    G = g_ref[...] + brow_ref[...]
    GT = gt_ref[...] + bcol_ref[...]
    row = lax.broadcasted_iota(jnp.int32, (L, L), 0)
    col = lax.broadcasted_iota(jnp.int32, (L, L), 1)
    causal = col <= row
    tri = causal.astype(BF16)
    tri_t = (row <= col).astype(BF16)
    bc = sum(jnp.dot(tri, part, preferred_element_type=F32) for part in _split3(_log_sigmoid(G)))
    br = sum(jnp.dot(part, tri_t, preferred_element_type=F32) for part in _split3(_log_sigmoid(GT)))
    ones_col = (lax.broadcasted_iota(jnp.int32, (L, dh), 1) == 0).astype(BF16)

    for h in range(HEADS):
        cols = slice(h * dh, (h + 1) * dh)
        q = q_ref[:, cols]
        k = k_ref[:, cols]
        v = v_ref[:, cols]
        b_col = bc[:, HEADS + h:HEADS + h + 1]
        b_row = br[HEADS + h:HEADS + h + 1, :]
        ig_col = G[:, h:h + 1]
        ig_row = GT[h:h + 1, :]
        m_prev = m_ref[h:h + 1, 0:1]
        c_ext = c_ref[h]

        D = jnp.where(causal, b_col - b_row + ig_row, NEG_BIG)
        inter = b_col + m_prev
        m_t = jnp.maximum(inter, jnp.max(D, axis=1, keepdims=True))
        w_intra = jnp.exp(D - m_t)
        w_inter = jnp.exp(inter - m_t)
        qk = lax.dot_general(q, k, (((1,), (1,)), ((), ())), preferred_element_type=F32)
        s = qk * w_intra
        qc = jnp.dot(q, c_ext.astype(BF16), preferred_element_type=F32)
        num = jnp.dot(s.astype(BF16), v, preferred_element_type=F32) + w_inter * qc[:, :dh]
        den = jnp.sum(s, axis=1, keepdims=True) + w_inter * qc[:, dh:dh + 1]
        hh = num / jnp.maximum(jnp.abs(den), jnp.exp(-m_t))
        hh = hh * lax.rsqrt(jnp.mean(hh * hh, axis=1, keepdims=True) + NORM_EPS)
        o = o_ref[:, cols]
        y_ref[:, cols] = (hh * gain_ref[:, cols] / (1.0 + jnp.exp(-o))).astype(BF16)

        b_last = b_col[L - 1:L, :]
        g_col = b_last - b_col + ig_col
        m_new = jnp.maximum(b_last + m_prev, jnp.max(g_col, axis=0, keepdims=True))
        decay = jnp.exp(b_last + m_prev - m_new)
        wk = jnp.exp(g_col - m_new)
        kw_t = (wk * k.astype(F32)).T.astype(BF16)
        v_ext = jnp.concatenate([v, ones_col], axis=1)
        c_ref[h] = decay * c_ext + jnp.dot(kw_t, v_ext, preferred_element_type=F32)
        m_ref[h:h + 1, :] = jnp.broadcast_to(m_new, (1, LANES))


def _mlstm(pf, pb, gates, gates_t, bias_row, bias_col, head_gain):
    S = pf.shape[0]
    L = MLSTM_CHUNK
    return pl.pallas_call(
        _mlstm_kernel,
        grid=(S // L,),
        in_specs=[pl.BlockSpec((L, GROUP), lambda c: (c, P_MQ)),
                  pl.BlockSpec((L, GROUP), lambda c: (c, P_MK)),
                  pl.BlockSpec((L, GROUP), lambda c: (c, P_MV)),
                  pl.BlockSpec((L, GROUP), lambda c: (c, P_MO)),
                  pl.BlockSpec((L, LANES), lambda c: (c, 0)),
                  pl.BlockSpec((LANES, L), lambda c: (0, c)),
                  pl.BlockSpec((1, LANES), lambda c: (0, 0)),
                  pl.BlockSpec((LANES, LANES), lambda c: (0, 0)),
                  pl.BlockSpec((1, GROUP), lambda c: (0, 0))],
        out_specs=pl.BlockSpec((L, GROUP), lambda c: (c, 0)),
        out_shape=jax.ShapeDtypeStruct((S, GROUP), BF16),
        scratch_shapes=[pltpu.VMEM((HEADS, HEAD_DIM, 2 * HEAD_DIM), F32), pltpu.VMEM((8, LANES), F32)],
        compiler_params=_params("arbitrary"),
        name="mlstm",
    )(pb, pb, pb, pf, gates, gates_t, bias_row, bias_col, head_gain)


def _softplus(z):
    return jnp.maximum(z, 0.0) + jnp.log(1.0 + jnp.exp(-jnp.abs(z)))


def _sb_kernel(q_ref, k_ref, v_ref, u_ref, y_ref, acc_ref, carry_ref, *, tq, tk):
    i = pl.program_id(1)
    q = q_ref[...]
    acc_ref[...] = jnp.zeros_like(acc_ref)
    carry_ref[...] = jnp.zeros_like(carry_ref)
    q_pos = i * tq + lax.broadcasted_iota(jnp.int32, (tq, tk), 0)
    k_off = lax.broadcasted_iota(jnp.int32, (tq, tk), 1)

    def block(j, masked):
        rows = pl.ds(pl.multiple_of(j * tk, tk), tk)
        kb = k_ref[rows, :]
        vb = v_ref[rows, :]
        z = lax.dot_general(q, kb, (((1,), (1,)), ((), ())), preferred_element_type=F32)
        sp = _softplus(z)
        if masked:
            past = (j * tk + k_off) < q_pos
            sp = jnp.where(past, sp, 0.0)
        hi = sp.astype(BF16)
        lo = (sp - hi.astype(F32)).astype(BF16)
        R = (jnp.dot(hi, u_ref[...], preferred_element_type=F32)
             + jnp.dot(lo, u_ref[...], preferred_element_type=F32))
        carry = carry_ref[...]
        A = jnp.exp(z - R - carry)
        if masked:
            A = jnp.where(past, A, 0.0)
        acc_ref[...] += jnp.dot(A.astype(BF16), vb, preferred_element_type=F32)
        carry_ref[...] = carry + R[:, 0:1]

    n_diag = tq // tk
    j_hi = (i + 1) * n_diag
    for d in range(n_diag):
        block(j_hi - 1 - d, True)

    def cond(state):
        jj, live = state
        return jnp.logical_and(jj < j_hi - n_diag, live)

    def body(state):
        jj, _ = state
        block(j_hi - n_diag - 1 - jj, False)
        return jj + 1, jnp.min(carry_ref[...]) < SB_DEAD_CARRY

    lax.while_loop(cond, body, (jnp.int32(0), jnp.min(carry_ref[...]) < SB_DEAD_CARRY))
    y_ref[...] = acc_ref[...].astype(BF16)


def _sb_attention(pb, tri_u, *, tq=256, tk=256):
    S = pb.shape[0]
    tq = min(tq, S)
    tk = min(tk, tq)
    dh = HEAD_DIM
    qcol, kcol, vcol = P_SQ * HEADS, P_SK * HEADS, P_SV * HEADS
    return pl.pallas_call(
        functools.partial(_sb_kernel, tq=tq, tk=tk),
        grid=(HEADS, S // tq),
        in_specs=[pl.BlockSpec((tq, dh), lambda h, i: (i, qcol + h)),
                  pl.BlockSpec((S, dh), lambda h, i: (0, kcol + h)),
                  pl.BlockSpec((S, dh), lambda h, i: (0, vcol + h)),
                  pl.BlockSpec((tk, tk), lambda h, i: (0, 0))],
        out_specs=pl.BlockSpec((tq, dh), lambda h, i: (i, h)),
        out_shape=jax.ShapeDtypeStruct((S, GROUP), BF16),
        scratch_shapes=[pltpu.VMEM((tq, dh), F32), pltpu.VMEM((tq, 1), F32)],
        compiler_params=_params("arbitrary", "arbitrary"),
        name="sb_attention",
    )(pb, pb, pb, tri_u)


def _pack_w_in(w_in):
    G = GROUP
    gate0 = 7 * G
    after = gate0 + 2 * HEADS

    def panel(start):
        return w_in[:, :, start:start + G]

    main = jnp.concatenate(
        [panel(0), panel(G), panel(2 * G), panel(6 * G), panel(after),
         panel(3 * G), panel(4 * G), panel(5 * G),
         panel(after + G), panel(after + 2 * G), panel(after + 3 * G)], axis=2).astype(BF16)
    gate = jnp.pad(w_in[:, :, gate0:after], ((0, 0), (0, 0), (0, LANES - 2 * HEADS))).astype(BF16)
    return main, gate, jnp.swapaxes(gate, 1, 2)


def _token_mix(h, x, layer, w_main, w_gate, w_gate_t, w_out, conv_w, pool_w, pool_scale, i_bias, f_bias,
               head_gain, g_post, g_next, tri_u):
    pf, pb, gates, gates_t = _in_proj(h, w_main, w_gate, w_gate_t, layer)
    y_conv, y_pool = _conv_pool(pf, conv_w, pool_w, pool_scale[None, :])
    bias = jnp.pad(jnp.concatenate([i_bias, f_bias]), (0, LANES - 2 * HEADS))
    bias_col = jnp.broadcast_to(bias[:, None], (LANES, LANES))
    y_mlstm = _mlstm(pf, pb, gates, gates_t, bias[None, :], bias_col, head_gain[None, :])
    y_sb = _sb_attention(pb, tri_u)
    return _proj_res([y_conv, y_mlstm, y_pool, y_sb], w_out, layer, x, g_post, g_next, res_scale=1.0)


def kernel(x, w_in, w_out, conv_w, pool_w, pool_scale, mlstm_i_bias, mlstm_f_bias, mlstm_head_gain,
           ffn1_w_gate, ffn1_w_up, ffn1_w_down, ffn2_w_gate, ffn2_w_up, ffn2_w_down, norm_gains):
    B, S, D = x.shape
    depth = w_in.shape[0]
    outs = []
    tk = min(256, S)
    tri_u = (jnp.arange(tk)[:, None] >= jnp.arange(tk)[None, :]).astype(BF16)
    w_main, w_gate, w_gate_t = _pack_w_in(w_in)
    w_out_bf = w_out.astype(BF16)
    wd1_bf = ffn1_w_down.astype(BF16)
    wd2_bf = ffn2_w_down.astype(BF16)
    for b in range(B):
        xb = x[b]
        h = _first_norm(xb, norm_gains[0, 0][None, :])
        for l in range(depth):
            g = norm_gains[l]
            g_after = norm_gains[l + 1, 0] if l + 1 < depth else g[0]
            act = _ffn_up(h, ffn1_w_gate, ffn1_w_up, l)
            xb, h = _proj_res([act], wd1_bf, l, xb, g[1][None, :], g[2][None, :], res_scale=0.5)
            xb, h = _token_mix(h, xb, l, w_main, w_gate, w_gate_t, w_out_bf, conv_w[l], pool_w[l],
                               pool_scale[l], mlstm_i_bias[l], mlstm_f_bias[l], mlstm_head_gain[l],
                               g[3][None, :], g[4][None, :], tri_u)
            act = _ffn_up(h, ffn2_w_gate, ffn2_w_up, l)
            xb, h = _proj_res([act], wd2_bf, l, xb, g[5][None, :], g_after[None, :], res_scale=0.5)
        outs.append(xb)
    return jnp.stack(outs, axis=0)
```

```python
import functools

import jax
import jax.numpy as jnp
from jax import lax
from jax.experimental import pallas as pl
from jax.experimental.pallas import tpu as pltpu

F32 = jnp.float32
BF16 = jnp.bfloat16

NORM_EPS = 1e-6
HEAD_DIM = 128
HEADS = 4
GROUP = HEADS * HEAD_DIM
CONV_WIDTH = 3
POOL_WINDOWS = (2, 4, 8, 16)
POOL_HALO = 16
MLSTM_CHUNK = 128
LANES = 128
NEG_BIG = -1e30
SB_DEAD_CARRY = 105.0

VMEM_LIMIT = 56 * 1024 * 1024

F32_PANELS = 5
BF16_PANELS = 6
P_CB, P_CC, P_CU, P_MO, P_PU = range(F32_PANELS)
P_MQ, P_MK, P_MV, P_SQ, P_SK, P_SV = range(BF16_PANELS)


def _params(*sem):
    return pltpu.CompilerParams(dimension_semantics=sem, vmem_limit_bytes=VMEM_LIMIT)


def _rms_rows(y, g):
    ms = jnp.mean(y * y, axis=-1, keepdims=True)
    return y * lax.rsqrt(ms + NORM_EPS) * g


def _split3(x):
    hi = x.astype(BF16)
    r1 = x - hi.astype(F32)
    mid = r1.astype(BF16)
    lo = (r1 - mid.astype(F32)).astype(BF16)
    return hi, mid, lo


def _norm_kernel(x_ref, g_ref, h_ref):
    h_ref[...] = _rms_rows(x_ref[...], g_ref[...]).astype(BF16)


def _first_norm(x, g):
    S, D = x.shape
    tm = 256
    return pl.pallas_call(
        _norm_kernel,
        grid=(S // tm,),
        in_specs=[pl.BlockSpec((tm, D), lambda i: (i, 0)),
                  pl.BlockSpec((1, D), lambda i: (0, 0))],
        out_specs=pl.BlockSpec((tm, D), lambda i: (i, 0)),
        out_shape=jax.ShapeDtypeStruct((S, D), BF16),
        compiler_params=_params("arbitrary"),
        name="first_norm",
    )(x, g)


def _ffn_up_kernel(h_ref, wg_ref, wu_ref, o_ref, wg_bf, wu_bf, *, row_chunk):
    @pl.when(pl.program_id(1) == 0)
    def _():
        wg_bf[...] = wg_ref[...].astype(BF16)
        wu_bf[...] = wu_ref[...].astype(BF16)

    def body(r, carry):
        rows = pl.ds(pl.multiple_of(r * row_chunk, row_chunk), row_chunk)
        h = h_ref[rows, :]
        g = jnp.dot(h, wg_bf[...], preferred_element_type=F32)
        u = jnp.dot(h, wu_bf[...], preferred_element_type=F32)
        o_ref[rows, :] = (g / (1.0 + jnp.exp(-g)) * u).astype(BF16)
        return carry

    lax.fori_loop(0, h_ref.shape[0] // row_chunk, body, 0)


def _ffn_up(h, wg, wu, layer, *, tm=2048, tn=512, row_chunk=512):
    S, D = h.shape
    N = wg.shape[2]
    tm = min(tm, S)
    return pl.pallas_call(
        functools.partial(_ffn_up_kernel, row_chunk=min(row_chunk, tm)),
        grid=(N // tn, S // tm),
        in_specs=[pl.BlockSpec((tm, D), lambda j, i: (i, 0)),
                  pl.BlockSpec((None, D, tn), lambda j, i: (layer, 0, j)),
                  pl.BlockSpec((None, D, tn), lambda j, i: (layer, 0, j))],
        out_specs=pl.BlockSpec((tm, tn), lambda j, i: (i, j)),
        out_shape=jax.ShapeDtypeStruct((S, N), BF16),
        scratch_shapes=[pltpu.VMEM((D, tn), BF16), pltpu.VMEM((D, tn), BF16)],
        compiler_params=_params("arbitrary", "arbitrary"),
        name="ffn_up",
    )(h, wg, wu)


def _proj_res_kernel(*refs, n_lhs, nk, res_scale, row_chunk, ep_chunk):
    a_refs = refs[:n_lhs]
    w_ref, x_ref, gp_ref, gn_ref, xo_ref, ho_ref = refs[n_lhs:]
    k = pl.program_id(1)
    tm = xo_ref.shape[0]
    kw = a_refs[0].shape[1]

    def accumulate(first):
        def body(r, carry):
            rows = pl.ds(pl.multiple_of(r * row_chunk, row_chunk), row_chunk)
            part = jnp.dot(a_refs[0][rows, :], w_ref[0:kw, :], preferred_element_type=F32)
            for g in range(1, n_lhs):
                part = part + jnp.dot(a_refs[g][rows, :], w_ref[g * kw:(g + 1) * kw, :],
                                      preferred_element_type=F32)
            if first:
                xo_ref[rows, :] = part
            else:
                xo_ref[rows, :] += part
            return carry
        lax.fori_loop(0, tm // row_chunk, body, 0)

    def epilogue():
        def body(r, carry):
            rows = pl.ds(pl.multiple_of(r * ep_chunk, ep_chunk), ep_chunk)
            xn = x_ref[rows, :] + res_scale * _rms_rows(xo_ref[rows, :], gp_ref[...])
            xo_ref[rows, :] = xn
            ho_ref[rows, :] = _rms_rows(xn, gn_ref[...]).astype(BF16)
            return carry
        lax.fori_loop(0, tm // ep_chunk, body, 0, unroll=2)

    if nk == 1:
        accumulate(True)
        epilogue()
    else:
        pl.when(k == 0)(lambda: accumulate(True))
        pl.when(k > 0)(lambda: accumulate(False))
        pl.when(k == nk - 1)(epilogue)


def _proj_res(lhs, w, layer, x, g_post, g_next, *, res_scale, tm=512, tk=1408, row_chunk=512, ep_chunk=64):
    S, D = x.shape
    tm = min(tm, S)
    n_lhs = len(lhs)
    kw = lhs[0].shape[1]
    if n_lhs > 1:
        tk = n_lhs * kw
    nk = w.shape[1] // tk
    return pl.pallas_call(
        functools.partial(_proj_res_kernel, n_lhs=n_lhs, nk=nk, res_scale=res_scale,
                          row_chunk=min(row_chunk, tm), ep_chunk=ep_chunk),
        grid=(S // tm, nk),
        in_specs=[pl.BlockSpec((tm, tk // n_lhs), lambda i, k: (i, k)) for _ in lhs] + [
            pl.BlockSpec((None, tk, D), lambda i, k: (layer, k, 0)),
            pl.BlockSpec((tm, D), lambda i, k: (i, 0)),
            pl.BlockSpec((1, D), lambda i, k: (0, 0)),
            pl.BlockSpec((1, D), lambda i, k: (0, 0))],
        out_specs=[pl.BlockSpec((tm, D), lambda i, k: (i, 0)),
                   pl.BlockSpec((tm, D), lambda i, k: (i, 0))],
        out_shape=[jax.ShapeDtypeStruct((S, D), F32), jax.ShapeDtypeStruct((S, D), BF16)],
        compiler_params=_params("arbitrary", "arbitrary"),
        name="proj_res",
    )(*lhs, w, x, g_post, g_next)


def _in_proj_kernel(h_ref, w_ref, wg_ref, wgt_ref, of_ref, ob_ref, g_ref, gt_ref, *, scale):
    h = h_ref[...]
    for p in range(F32_PANELS):
        cols = slice(p * GROUP, (p + 1) * GROUP)
        of_ref[:, cols] = jnp.dot(h, w_ref[:, cols], preferred_element_type=F32)
    for p in range(BF16_PANELS):
        cols = slice(p * GROUP, (p + 1) * GROUP)
        wcols = slice((F32_PANELS + p) * GROUP, (F32_PANELS + p + 1) * GROUP)
        r = jnp.dot(h, w_ref[:, wcols], preferred_element_type=F32)
        if p in (P_MK, P_SQ):
            r = r * scale
        ob_ref[:, cols] = r.astype(BF16)
    g_ref[...] = jnp.dot(h, wg_ref[...], preferred_element_type=F32)
    gt_ref[...] = lax.dot_general(wgt_ref[...], h, (((1,), (1,)), ((), ())), preferred_element_type=F32)


def _in_proj(h, w_main, w_gate, w_gate_t, layer, *, tm=512):
    S, D = h.shape
    tm = min(tm, S)
    nf, nb = F32_PANELS * GROUP, BF16_PANELS * GROUP
    return pl.pallas_call(
        functools.partial(_in_proj_kernel, scale=HEAD_DIM ** -0.5),
        grid=(S // tm,),
        in_specs=[pl.BlockSpec((tm, D), lambda i: (i, 0)),
                  pl.BlockSpec((None, D, nf + nb), lambda i: (layer, 0, 0), pipeline_mode=pl.Buffered(1)),
                  pl.BlockSpec((None, D, LANES), lambda i: (layer, 0, 0)),
                  pl.BlockSpec((None, LANES, D), lambda i: (layer, 0, 0))],
        out_specs=[pl.BlockSpec((tm, nf), lambda i: (i, 0)),
                   pl.BlockSpec((tm, nb), lambda i: (i, 0)),
                   pl.BlockSpec((tm, LANES), lambda i: (i, 0)),
                   pl.BlockSpec((LANES, tm), lambda i: (0, i))],
        out_shape=[jax.ShapeDtypeStruct((S, nf), F32),
                   jax.ShapeDtypeStruct((S, nb), BF16),
                   jax.ShapeDtypeStruct((S, LANES), F32),
                   jax.ShapeDtypeStruct((LANES, S), F32)],
        compiler_params=_params("arbitrary"),
        name="in_proj",
    )(h, w_main, w_gate, w_gate_t)


def _conv_pool_kernel(cb_ref, cc_ref, cu_ref, pu_ref, cch_ref, cuh_ref, puh_ref,
                      cw_ref, pw_ref, ps_ref, yc_ref, yp_ref, zbuf, pbuf):
    i = pl.program_id(0)
    tm = cb_ref.shape[0]
    keep = (i > 0).astype(F32)
    H = POOL_HALO

    zbuf[0:H, :] = cch_ref[...] * cuh_ref[...] * keep
    zbuf[H:H + tm, :] = cc_ref[...] * cu_ref[...]
    y = cw_ref[0:1, :] * zbuf[H - 2:H - 2 + tm, :]
    y = y + cw_ref[1:2, :] * zbuf[H - 1:H - 1 + tm, :]
    y = y + cw_ref[2:3, :] * zbuf[H:H + tm, :]
    yc_ref[...] = (cb_ref[...] * y).astype(BF16)

    pbuf[0:H, :] = puh_ref[...] * keep
    pbuf[H:H + tm, :] = pu_ref[...]
    t = i * tm + lax.broadcasted_iota(jnp.int32, (tm, 1), 0)
    for g, win in enumerate(POOL_WINDOWS):
        cols = slice(g * LANES, (g + 1) * LANES)
        u = pbuf[H:H + tm, cols]
        wsum = u
        for d in range(1, win):
            wsum = wsum + pbuf[H - d:H - d + tm, cols]
        count = jnp.minimum(t + 1, win).astype(F32)
        pooled = wsum / count - u
        yg = jnp.dot(pooled.astype(BF16), pw_ref[g].astype(BF16), preferred_element_type=F32)
        yp_ref[:, cols] = (yg * ps_ref[:, cols]).astype(BF16)


def _conv_pool(pf, conv_w, pool_w, pool_scale, *, tm=512):
    S = pf.shape[0]
    tm = min(tm, S)
    H = POOL_HALO
    hb = tm // H

    def cur(p):
        return pl.BlockSpec((tm, GROUP), lambda i, p=p: (i, p))

    def halo(p):
        return pl.BlockSpec((H, GROUP), lambda i, p=p: (jnp.maximum(i * hb - 1, 0), p))

    return pl.pallas_call(
        _conv_pool_kernel,
        grid=(S // tm,),
        in_specs=[cur(P_CB), cur(P_CC), cur(P_CU), cur(P_PU), halo(P_CC), halo(P_CU), halo(P_PU),
                  pl.BlockSpec((CONV_WIDTH, GROUP), lambda i: (0, 0)),
                  pl.BlockSpec(pool_w.shape, lambda i: (0, 0, 0)),
                  pl.BlockSpec((1, GROUP), lambda i: (0, 0))],
        out_specs=[pl.BlockSpec((tm, GROUP), lambda i: (i, 0)),
                   pl.BlockSpec((tm, GROUP), lambda i: (i, 0))],
        out_shape=[jax.ShapeDtypeStruct((S, GROUP), BF16), jax.ShapeDtypeStruct((S, GROUP), BF16)],
        scratch_shapes=[pltpu.VMEM((H + tm, GROUP), F32), pltpu.VMEM((H + tm, GROUP), F32)],
        compiler_params=_params("arbitrary"),
        name="conv_pool",
    )(pf, pf, pf, pf, pf, pf, pf, conv_w, pool_w, pool_scale)


def _log_sigmoid(x):
    return jnp.minimum(x, 0.0) - jnp.log(1.0 + jnp.exp(-jnp.abs(x)))


def _mlstm_kernel(q_ref, k_ref, v_ref, o_ref, g_ref, gt_ref, brow_ref, bcol_ref, gain_ref,
                  y_ref, c_ref, m_ref, *, n_sub):
    L = MLSTM_CHUNK
    dh = HEAD_DIM

    @pl.when(pl.program_id(0) == 0)
    def _():
        c_ref[...] = jnp.zeros_like(c_ref)
        m_ref[...] = jnp.zeros_like(m_ref)

    row = lax.broadcasted_iota(jnp.int32, (L, L), 0)
    col = lax.broadcasted_iota(jnp.int32, (L, L), 1)
    causal = col <= row
    tri = causal.astype(BF16)
    tri_t = (row <= col).astype(BF16)
    ones_col = (lax.broadcasted_iota(jnp.int32, (L, dh), 1) == 0).astype(BF16)

    for sub in range(n_sub):
        rows = slice(sub * L, (sub + 1) * L)
        G = g_ref[rows, :] + brow_ref[...]
        GT = gt_ref[:, rows] + bcol_ref[...]
        bc = sum(jnp.dot(tri, part, preferred_element_type=F32) for part in _split3(_log_sigmoid(G)))
        br = sum(jnp.dot(part, tri_t, preferred_element_type=F32) for part in _split3(_log_sigmoid(GT)))

        for h in range(HEADS):
            cols = slice(h * dh, (h + 1) * dh)
            q = q_ref[rows, cols]
            k = k_ref[rows, cols]
            v = v_ref[rows, cols]
            b_col = bc[:, HEADS + h:HEADS + h + 1]
            b_row = br[HEADS + h:HEADS + h + 1, :]
            ig_col = G[:, h:h + 1]
            ig_row = GT[h:h + 1, :]
            m_prev = m_ref[h:h + 1, 0:1]
            c_ext = c_ref[h]

            D = jnp.where(causal, b_col - b_row + ig_row, NEG_BIG)
            inter = b_col + m_prev
            m_t = jnp.maximum(inter, jnp.max(D, axis=1, keepdims=True))
            w_intra = jnp.exp(D - m_t)
            w_inter = jnp.exp(inter - m_t)
            qk = lax.dot_general(q, k, (((1,), (1,)), ((), ())), preferred_element_type=F32)
            s = qk * w_intra
            qc = jnp.dot(q, c_ext.astype(BF16), preferred_element_type=F32)
            num = jnp.dot(s.astype(BF16), v, preferred_element_type=F32) + w_inter * qc[:, :dh]
            den = jnp.sum(s, axis=1, keepdims=True) + w_inter * qc[:, dh:dh + 1]
            hh = num / jnp.maximum(jnp.abs(den), jnp.exp(-m_t))
            hh = hh * lax.rsqrt(jnp.mean(hh * hh, axis=1, keepdims=True) + NORM_EPS)
            o = o_ref[rows, cols]
            y_ref[rows, cols] = (hh * gain_ref[:, cols] / (1.0 + jnp.exp(-o))).astype(BF16)

            b_last = b_col[L - 1:L, :]
            g_col = b_last - b_col + ig_col
            m_new = jnp.maximum(b_last + m_prev, jnp.max(g_col, axis=0, keepdims=True))
            decay = jnp.exp(b_last + m_prev - m_new)
            wk = jnp.exp(g_col - m_new)
            kw_t = (wk * k.astype(F32)).T.astype(BF16)
            v_ext = jnp.concatenate([v, ones_col], axis=1)
            c_ref[h] = decay * c_ext + jnp.dot(kw_t, v_ext, preferred_element_type=F32)
            m_ref[h:h + 1, :] = jnp.broadcast_to(m_new, (1, LANES))


def _mlstm(pf, pb, gates, gates_t, bias_row, bias_col, head_gain, *, n_sub=1):
    S = pf.shape[0]
    L = n_sub * MLSTM_CHUNK
    return pl.pallas_call(
        functools.partial(_mlstm_kernel, n_sub=n_sub),
        grid=(S // L,),
        in_specs=[pl.BlockSpec((L, GROUP), lambda c: (c, P_MQ)),
                  pl.BlockSpec((L, GROUP), lambda c: (c, P_MK)),
                  pl.BlockSpec((L, GROUP), lambda c: (c, P_MV)),
                  pl.BlockSpec((L, GROUP), lambda c: (c, P_MO)),
                  pl.BlockSpec((L, LANES), lambda c: (c, 0)),
                  pl.BlockSpec((LANES, L), lambda c: (0, c)),
                  pl.BlockSpec((1, LANES), lambda c: (0, 0)),
                  pl.BlockSpec((LANES, LANES), lambda c: (0, 0)),
                  pl.BlockSpec((1, GROUP), lambda c: (0, 0))],
        out_specs=pl.BlockSpec((L, GROUP), lambda c: (c, 0)),
        out_shape=jax.ShapeDtypeStruct((S, GROUP), BF16),
        scratch_shapes=[pltpu.VMEM((HEADS, HEAD_DIM, 2 * HEAD_DIM), F32), pltpu.VMEM((8, LANES), F32)],
        compiler_params=_params("arbitrary"),
        name="mlstm",
    )(pb, pb, pb, pf, gates, gates_t, bias_row, bias_col, head_gain)


def _softplus(z):
    return jnp.maximum(z, 0.0) + jnp.log(1.0 + jnp.exp(-jnp.abs(z)))


def _sb_kernel(q_ref, k_ref, v_ref, u_ref, y_ref, acc_ref, carry_ref, *, tq, tk):
    i = pl.program_id(1)
    q = q_ref[...]
    acc_ref[...] = jnp.zeros_like(acc_ref)
    carry_ref[...] = jnp.zeros_like(carry_ref)
    q_pos = i * tq + lax.broadcasted_iota(jnp.int32, (tq, tk), 0)
    k_off = lax.broadcasted_iota(jnp.int32, (tq, tk), 1)

    def block(j, masked):
        rows = pl.ds(pl.multiple_of(j * tk, tk), tk)
        kb = k_ref[rows, :]
        vb = v_ref[rows, :]
        z = lax.dot_general(q, kb, (((1,), (1,)), ((), ())), preferred_element_type=F32)
        sp = _softplus(z)
        if masked:
            past = (j * tk + k_off) < q_pos
            sp = jnp.where(past, sp, 0.0)
        hi = sp.astype(BF16)
        lo = (sp - hi.astype(F32)).astype(BF16)
        R = (jnp.dot(hi, u_ref[...], preferred_element_type=F32)
             + jnp.dot(lo, u_ref[...], preferred_element_type=F32))
        carry = carry_ref[...]
        A = jnp.exp(z - R - carry)
        if masked:
            A = jnp.where(past, A, 0.0)
        acc_ref[...] += jnp.dot(A.astype(BF16), vb, preferred_element_type=F32)
        carry_ref[...] = carry + R[:, 0:1]

    n_diag = tq // tk
    j_hi = (i + 1) * n_diag
    for d in range(n_diag):
        block(j_hi - 1 - d, True)

    def cond(state):
        jj, live = state
        return jnp.logical_and(jj < j_hi - n_diag, live)

    def body(state):
        jj, _ = state
        block(j_hi - n_diag - 1 - jj, False)
        return jj + 1, jnp.min(carry_ref[...]) < SB_DEAD_CARRY

    lax.while_loop(cond, body, (jnp.int32(0), jnp.min(carry_ref[...]) < SB_DEAD_CARRY))
    y_ref[...] = acc_ref[...].astype(BF16)


def _sb_attention(pb, tri_u, *, tq=512, tk=256):
    S = pb.shape[0]
    tq = min(tq, S)
    tk = min(tk, tq)
    dh = HEAD_DIM
    qcol, kcol, vcol = P_SQ * HEADS, P_SK * HEADS, P_SV * HEADS
    return pl.pallas_call(
        functools.partial(_sb_kernel, tq=tq, tk=tk),
        grid=(HEADS, S // tq),
        in_specs=[pl.BlockSpec((tq, dh), lambda h, i: (i, qcol + h)),
                  pl.BlockSpec((S, dh), lambda h, i: (0, kcol + h)),
                  pl.BlockSpec((S, dh), lambda h, i: (0, vcol + h)),
                  pl.BlockSpec((tk, tk), lambda h, i: (0, 0))],
        out_specs=pl.BlockSpec((tq, dh), lambda h, i: (i, h)),
        out_shape=jax.ShapeDtypeStruct((S, GROUP), BF16),
        scratch_shapes=[pltpu.VMEM((tq, dh), F32), pltpu.VMEM((tq, 1), F32)],
        compiler_params=_params("arbitrary", "arbitrary"),
        name="sb_attention",
    )(pb, pb, pb, tri_u)


def _pack_w_in(w_in):
    G = GROUP
    gate0 = 7 * G
    after = gate0 + 2 * HEADS

    def panel(start):
        return w_in[:, :, start:start + G]

    main = jnp.concatenate(
        [panel(0), panel(G), panel(2 * G), panel(6 * G), panel(after),
         panel(3 * G), panel(4 * G), panel(5 * G),
         panel(after + G), panel(after + 2 * G), panel(after + 3 * G)], axis=2).astype(BF16)
    gate = jnp.pad(w_in[:, :, gate0:after], ((0, 0), (0, 0), (0, LANES - 2 * HEADS))).astype(BF16)
    return main, gate, jnp.swapaxes(gate, 1, 2)


def _token_mix(h, x, layer, w_main, w_gate, w_gate_t, w_out, conv_w, pool_w, pool_scale, i_bias, f_bias,
               head_gain, g_post, g_next, tri_u):
    pf, pb, gates, gates_t = _in_proj(h, w_main, w_gate, w_gate_t, layer)
    y_conv, y_pool = _conv_pool(pf, conv_w, pool_w, pool_scale[None, :])
    bias = jnp.pad(jnp.concatenate([i_bias, f_bias]), (0, LANES - 2 * HEADS))
    bias_col = jnp.broadcast_to(bias[:, None], (LANES, LANES))
    y_mlstm = _mlstm(pf, pb, gates, gates_t, bias[None, :], bias_col, head_gain[None, :])
    y_sb = _sb_attention(pb, tri_u)
    return _proj_res([y_conv, y_mlstm, y_pool, y_sb], w_out, layer, x, g_post, g_next, res_scale=1.0)


def kernel(x, w_in, w_out, conv_w, pool_w, pool_scale, mlstm_i_bias, mlstm_f_bias, mlstm_head_gain,
           ffn1_w_gate, ffn1_w_up, ffn1_w_down, ffn2_w_gate, ffn2_w_up, ffn2_w_down, norm_gains):
    B, S, D = x.shape
    depth = w_in.shape[0]
    outs = []
    tk = min(256, S)
    tri_u = (jnp.arange(tk)[:, None] >= jnp.arange(tk)[None, :]).astype(BF16)
    w_main, w_gate, w_gate_t = _pack_w_in(w_in)
    w_out_bf = w_out.astype(BF16)
    wd1_bf = ffn1_w_down.astype(BF16)
    wd2_bf = ffn2_w_down.astype(BF16)
    for b in range(B):
        xb = x[b]
        h = _first_norm(xb, norm_gains[0, 0][None, :])
        for l in range(depth):
            g = norm_gains[l]
            g_after = norm_gains[l + 1, 0] if l + 1 < depth else g[0]
            act = _ffn_up(h, ffn1_w_gate, ffn1_w_up, l)
            xb, h = _proj_res([act], wd1_bf, l, xb, g[1][None, :], g[2][None, :], res_scale=0.5)
            xb, h = _token_mix(h, xb, l, w_main, w_gate, w_gate_t, w_out_bf, conv_w[l], pool_w[l],
                               pool_scale[l], mlstm_i_bias[l], mlstm_f_bias[l], mlstm_head_gain[l],
                               g[3][None, :], g[4][None, :], tri_u)
            act = _ffn_up(h, ffn2_w_gate, ffn2_w_up, l)
            xb, h = _proj_res([act], wd2_bf, l, xb, g[5][None, :], g_after[None, :], res_scale=0.5)
        outs.append(xb)
    return jnp.stack(outs, axis=0)
```

```python
import functools

import jax
import jax.numpy as jnp
from jax import lax
from jax.experimental import pallas as pl
from jax.experimental.pallas import tpu as pltpu

F32 = jnp.float32
BF16 = jnp.bfloat16

NORM_EPS = 1e-6
HEAD_DIM = 128
HEADS = 4
GROUP = HEADS * HEAD_DIM
CONV_WIDTH = 3
POOL_WINDOWS = (2, 4, 8, 16)
POOL_HALO = 16
MLSTM_CHUNK = 128
LANES = 128
NEG_BIG = -1e30
SB_DEAD_CARRY = 105.0

VMEM_LIMIT = 56 * 1024 * 1024

F32_PANELS = 5
BF16_PANELS = 6
P_CB, P_CC, P_CU, P_MO, P_PU = range(F32_PANELS)
P_MQ, P_MK, P_MV, P_SQ, P_SK, P_SV = range(BF16_PANELS)


def _params(*sem):
    return pltpu.CompilerParams(dimension_semantics=sem, vmem_limit_bytes=VMEM_LIMIT)


def _rms_rows(y, g):
    ms = jnp.mean(y * y, axis=-1, keepdims=True)
    return y * lax.rsqrt(ms + NORM_EPS) * g


def _split3(x):
    hi = x.astype(BF16)
    r1 = x - hi.astype(F32)
    mid = r1.astype(BF16)
    lo = (r1 - mid.astype(F32)).astype(BF16)
    return hi, mid, lo


def _norm_kernel(x_ref, g_ref, h_ref):
    h_ref[...] = _rms_rows(x_ref[...], g_ref[...]).astype(BF16)


def _first_norm(x, g):
    S, D = x.shape
    tm = 256
    return pl.pallas_call(
        _norm_kernel,
        grid=(S // tm,),
        in_specs=[pl.BlockSpec((tm, D), lambda i: (i, 0)),
                  pl.BlockSpec((1, D), lambda i: (0, 0))],
        out_specs=pl.BlockSpec((tm, D), lambda i: (i, 0)),
        out_shape=jax.ShapeDtypeStruct((S, D), BF16),
        compiler_params=_params("arbitrary"),
        name="first_norm",
    )(x, g)


def _ffn_up_kernel(h_ref, wg_ref, wu_ref, o_ref, wg_bf, wu_bf, *, row_chunk):
    @pl.when(pl.program_id(1) == 0)
    def _():
        wg_bf[...] = wg_ref[...].astype(BF16)
        wu_bf[...] = wu_ref[...].astype(BF16)

    def body(r, carry):
        rows = pl.ds(pl.multiple_of(r * row_chunk, row_chunk), row_chunk)
        h = h_ref[rows, :]
        g = jnp.dot(h, wg_bf[...], preferred_element_type=F32)
        u = jnp.dot(h, wu_bf[...], preferred_element_type=F32)
        o_ref[rows, :] = (g / (1.0 + jnp.exp(-g)) * u).astype(BF16)
        return carry

    lax.fori_loop(0, h_ref.shape[0] // row_chunk, body, 0)


def _ffn_up(h, wg, wu, layer, *, tm=2048, tn=512, row_chunk=512):
    S, D = h.shape
    N = wg.shape[2]
    tm = min(tm, S)
    return pl.pallas_call(
        functools.partial(_ffn_up_kernel, row_chunk=min(row_chunk, tm)),
        grid=(N // tn, S // tm),
        in_specs=[pl.BlockSpec((tm, D), lambda j, i: (i, 0)),
                  pl.BlockSpec((None, D, tn), lambda j, i: (layer, 0, j)),
                  pl.BlockSpec((None, D, tn), lambda j, i: (layer, 0, j))],
        out_specs=pl.BlockSpec((tm, tn), lambda j, i: (i, j)),
        out_shape=jax.ShapeDtypeStruct((S, N), BF16),
        scratch_shapes=[pltpu.VMEM((D, tn), BF16), pltpu.VMEM((D, tn), BF16)],
        compiler_params=_params("arbitrary", "arbitrary"),
        name="ffn_up",
    )(h, wg, wu)


def _proj_res_kernel(*refs, n_lhs, nk, res_scale, row_chunk, ep_chunk):
    a_refs = refs[:n_lhs]
    w_ref, x_ref, gp_ref, gn_ref, xo_ref, ho_ref = refs[n_lhs:]
    k = pl.program_id(1)
    tm = xo_ref.shape[0]
    kw = a_refs[0].shape[1]

    def accumulate(first):
        def body(r, carry):
            rows = pl.ds(pl.multiple_of(r * row_chunk, row_chunk), row_chunk)
            part = jnp.dot(a_refs[0][rows, :], w_ref[0:kw, :], preferred_element_type=F32)
            for g in range(1, n_lhs):
                part = part + jnp.dot(a_refs[g][rows, :], w_ref[g * kw:(g + 1) * kw, :],
                                      preferred_element_type=F32)
            if first:
                xo_ref[rows, :] = part
            else:
                xo_ref[rows, :] += part
            return carry
        lax.fori_loop(0, tm // row_chunk, body, 0)

    def epilogue():
        def body(r, carry):
            rows = pl.ds(pl.multiple_of(r * ep_chunk, ep_chunk), ep_chunk)
            xn = x_ref[rows, :] + res_scale * _rms_rows(xo_ref[rows, :], gp_ref[...])
            xo_ref[rows, :] = xn
            ho_ref[rows, :] = _rms_rows(xn, gn_ref[...]).astype(BF16)
            return carry
        lax.fori_loop(0, tm // ep_chunk, body, 0, unroll=2)

    if nk == 1:
        accumulate(True)
        epilogue()
    else:
        pl.when(k == 0)(lambda: accumulate(True))
        pl.when(k > 0)(lambda: accumulate(False))
        pl.when(k == nk - 1)(epilogue)


def _proj_res(lhs, w, layer, x, g_post, g_next, *, res_scale, tm=512, tk=1408, row_chunk=512, ep_chunk=64):
    S, D = x.shape
    tm = min(tm, S)
    n_lhs = len(lhs)
    kw = lhs[0].shape[1]
    if n_lhs > 1:
        tk = n_lhs * kw
    nk = w.shape[1] // tk
    return pl.pallas_call(
        functools.partial(_proj_res_kernel, n_lhs=n_lhs, nk=nk, res_scale=res_scale,
                          row_chunk=min(row_chunk, tm), ep_chunk=ep_chunk),
        grid=(S // tm, nk),
        in_specs=[pl.BlockSpec((tm, tk // n_lhs), lambda i, k: (i, k)) for _ in lhs] + [
            pl.BlockSpec((None, tk, D), lambda i, k: (layer, k, 0)),
            pl.BlockSpec((tm, D), lambda i, k: (i, 0)),
            pl.BlockSpec((1, D), lambda i, k: (0, 0)),
            pl.BlockSpec((1, D), lambda i, k: (0, 0))],
        out_specs=[pl.BlockSpec((tm, D), lambda i, k: (i, 0)),
                   pl.BlockSpec((tm, D), lambda i, k: (i, 0))],
        out_shape=[jax.ShapeDtypeStruct((S, D), F32), jax.ShapeDtypeStruct((S, D), BF16)],
        compiler_params=_params("arbitrary", "arbitrary"),
        name="proj_res",
    )(*lhs, w, x, g_post, g_next)


def _in_proj_kernel(h_ref, w_ref, wg_ref, wgt_ref, of_ref, ob_ref, g_ref, gt_ref, *, scale):
    h = h_ref[...]
    for p in range(F32_PANELS):
        cols = slice(p * GROUP, (p + 1) * GROUP)
        of_ref[:, cols] = jnp.dot(h, w_ref[:, cols], preferred_element_type=F32)
    for p in range(BF16_PANELS):
        cols = slice(p * GROUP, (p + 1) * GROUP)
        wcols = slice((F32_PANELS + p) * GROUP, (F32_PANELS + p + 1) * GROUP)
        r = jnp.dot(h, w_ref[:, wcols], preferred_element_type=F32)
        if p in (P_MK, P_SQ):
            r = r * scale
        ob_ref[:, cols] = r.astype(BF16)
    g_ref[...] = jnp.dot(h, wg_ref[...], preferred_element_type=F32)
    gt_ref[...] = lax.dot_general(wgt_ref[...], h, (((1,), (1,)), ((), ())), preferred_element_type=F32)


def _in_proj(h, w_main, w_gate, w_gate_t, layer, *, tm=512):
    S, D = h.shape
    tm = min(tm, S)
    nf, nb = F32_PANELS * GROUP, BF16_PANELS * GROUP
    return pl.pallas_call(
        functools.partial(_in_proj_kernel, scale=HEAD_DIM ** -0.5),
        grid=(S // tm,),
        in_specs=[pl.BlockSpec((tm, D), lambda i: (i, 0)),
                  pl.BlockSpec((None, D, nf + nb), lambda i: (layer, 0, 0), pipeline_mode=pl.Buffered(1)),
                  pl.BlockSpec((None, D, LANES), lambda i: (layer, 0, 0)),
                  pl.BlockSpec((None, LANES, D), lambda i: (layer, 0, 0))],
        out_specs=[pl.BlockSpec((tm, nf), lambda i: (i, 0)),
                   pl.BlockSpec((tm, nb), lambda i: (i, 0)),
                   pl.BlockSpec((tm, LANES), lambda i: (i, 0)),
                   pl.BlockSpec((LANES, tm), lambda i: (0, i))],
        out_shape=[jax.ShapeDtypeStruct((S, nf), F32),
                   jax.ShapeDtypeStruct((S, nb), BF16),
                   jax.ShapeDtypeStruct((S, LANES), F32),
                   jax.ShapeDtypeStruct((LANES, S), F32)],
        compiler_params=_params("arbitrary"),
        name="in_proj",
    )(h, w_main, w_gate, w_gate_t)


def _conv_pool_kernel(cb_ref, cc_ref, cu_ref, pu_ref, cch_ref, cuh_ref, puh_ref,
                      cw_ref, pw_ref, ps_ref, yc_ref, yp_ref, zbuf, pbuf):
    i = pl.program_id(0)
    tm = cb_ref.shape[0]
    keep = (i > 0).astype(F32)
    H = POOL_HALO

    zbuf[0:H, :] = cch_ref[...] * cuh_ref[...] * keep
    zbuf[H:H + tm, :] = cc_ref[...] * cu_ref[...]
    y = cw_ref[0:1, :] * zbuf[H - 2:H - 2 + tm, :]
    y = y + cw_ref[1:2, :] * zbuf[H - 1:H - 1 + tm, :]
    y = y + cw_ref[2:3, :] * zbuf[H:H + tm, :]
    yc_ref[...] = (cb_ref[...] * y).astype(BF16)

    pbuf[0:H, :] = puh_ref[...] * keep
    pbuf[H:H + tm, :] = pu_ref[...]
    t = i * tm + lax.broadcasted_iota(jnp.int32, (tm, 1), 0)
    for g, win in enumerate(POOL_WINDOWS):
        cols = slice(g * LANES, (g + 1) * LANES)
        u = pbuf[H:H + tm, cols]
        wsum = u
        for d in range(1, win):
            wsum = wsum + pbuf[H - d:H - d + tm, cols]
        count = jnp.minimum(t + 1, win).astype(F32)
        pooled = wsum / count - u
        yg = jnp.dot(pooled.astype(BF16), pw_ref[g].astype(BF16), preferred_element_type=F32)
        yp_ref[:, cols] = (yg * ps_ref[:, cols]).astype(BF16)


def _conv_pool(pf, conv_w, pool_w, pool_scale, *, tm=512):
    S = pf.shape[0]
    tm = min(tm, S)
    H = POOL_HALO
    hb = tm // H

    def cur(p):
        return pl.BlockSpec((tm, GROUP), lambda i, p=p: (i, p))

    def halo(p):
        return pl.BlockSpec((H, GROUP), lambda i, p=p: (jnp.maximum(i * hb - 1, 0), p))

    return pl.pallas_call(
        _conv_pool_kernel,
        grid=(S // tm,),
        in_specs=[cur(P_CB), cur(P_CC), cur(P_CU), cur(P_PU), halo(P_CC), halo(P_CU), halo(P_PU),
                  pl.BlockSpec((CONV_WIDTH, GROUP), lambda i: (0, 0)),
                  pl.BlockSpec(pool_w.shape, lambda i: (0, 0, 0)),
                  pl.BlockSpec((1, GROUP), lambda i: (0, 0))],
        out_specs=[pl.BlockSpec((tm, GROUP), lambda i: (i, 0)),
                   pl.BlockSpec((tm, GROUP), lambda i: (i, 0))],
        out_shape=[jax.ShapeDtypeStruct((S, GROUP), BF16), jax.ShapeDtypeStruct((S, GROUP), BF16)],
        scratch_shapes=[pltpu.VMEM((H + tm, GROUP), F32), pltpu.VMEM((H + tm, GROUP), F32)],
        compiler_params=_params("arbitrary"),
        name="conv_pool",
    )(pf, pf, pf, pf, pf, pf, pf, conv_w, pool_w, pool_scale)


def _log_sigmoid(x):
    return jnp.minimum(x, 0.0) - jnp.log(1.0 + jnp.exp(-jnp.abs(x)))


def _mlstm_kernel(q_ref, k_ref, v_ref, o_ref, g_ref, gt_ref, brow_ref, bcol_ref, gain_ref,
                  y_ref, c_ref, m_ref, *, n_sub):
    L = MLSTM_CHUNK
    dh = HEAD_DIM

    @pl.when(pl.program_id(0) == 0)
    def _():
        c_ref[...] = jnp.zeros_like(c_ref)
        m_ref[...] = jnp.zeros_like(m_ref)

    row = lax.broadcasted_iota(jnp.int32, (L, L), 0)
    col = lax.broadcasted_iota(jnp.int32, (L, L), 1)
    causal = col <= row
    tri = causal.astype(BF16)
    tri_t = (row <= col).astype(BF16)
    ones_col = (lax.broadcasted_iota(jnp.int32, (L, dh), 1) == 0).astype(BF16)

    for sub in range(n_sub):
        rows = slice(sub * L, (sub + 1) * L)
        G = g_ref[rows, :] + brow_ref[...]
        GT = gt_ref[:, rows] + bcol_ref[...]
        bc = sum(jnp.dot(tri, part, preferred_element_type=F32) for part in _split3(_log_sigmoid(G)))
        br = sum(jnp.dot(part, tri_t, preferred_element_type=F32) for part in _split3(_log_sigmoid(GT)))

        y_new, c_new, m_new_rows = [], [], []
        for h in range(HEADS):
            cols = slice(h * dh, (h + 1) * dh)
            q = q_ref[rows, cols]
            k = k_ref[rows, cols]
            v = v_ref[rows, cols]
            b_col = bc[:, HEADS + h:HEADS + h + 1]
            b_row = br[HEADS + h:HEADS + h + 1, :]
            ig_col = G[:, h:h + 1]
            ig_row = GT[h:h + 1, :]
            m_prev = m_ref[h:h + 1, 0:1]
            c_ext = c_ref[h]

            D = jnp.where(causal, b_col - b_row + ig_row, NEG_BIG)
            inter = b_col + m_prev
            m_t = jnp.maximum(inter, jnp.max(D, axis=1, keepdims=True))
            w_intra = jnp.exp(D - m_t)
            w_inter = jnp.exp(inter - m_t)
            qk = lax.dot_general(q, k, (((1,), (1,)), ((), ())), preferred_element_type=F32)
            s = qk * w_intra
            qc = jnp.dot(q, c_ext.astype(BF16), preferred_element_type=F32)
            num = jnp.dot(s.astype(BF16), v, preferred_element_type=F32) + w_inter * qc[:, :dh]
            den = jnp.sum(s, axis=1, keepdims=True) + w_inter * qc[:, dh:dh + 1]
            hh = num / jnp.maximum(jnp.abs(den), jnp.exp(-m_t))
            hh = hh * lax.rsqrt(jnp.mean(hh * hh, axis=1, keepdims=True) + NORM_EPS)
            o = o_ref[rows, cols]
            y_new.append((hh * gain_ref[:, cols] / (1.0 + jnp.exp(-o))).astype(BF16))

            b_last = b_col[L - 1:L, :]
            g_col = b_last - b_col + ig_col
            m_new = jnp.maximum(b_last + m_prev, jnp.max(g_col, axis=0, keepdims=True))
            decay = jnp.exp(b_last + m_prev - m_new)
            wk = jnp.exp(g_col - m_new)
            kw_t = (wk * k.astype(F32)).T.astype(BF16)
            v_ext = jnp.concatenate([v, ones_col], axis=1)
            c_new.append(decay * c_ext + jnp.dot(kw_t, v_ext, preferred_element_type=F32))
            m_new_rows.append(jnp.broadcast_to(m_new, (1, LANES)))

        y_ref[rows, :] = jnp.concatenate(y_new, axis=1)
        for h in range(HEADS):
            c_ref[h] = c_new[h]
            m_ref[h:h + 1, :] = m_new_rows[h]


def _mlstm(pf, pb, gates, gates_t, bias_row, bias_col, head_gain, *, n_sub=1):
    S = pf.shape[0]
    L = n_sub * MLSTM_CHUNK
    return pl.pallas_call(
        functools.partial(_mlstm_kernel, n_sub=n_sub),
        grid=(S // L,),
        in_specs=[pl.BlockSpec((L, GROUP), lambda c: (c, P_MQ)),
                  pl.BlockSpec((L, GROUP), lambda c: (c, P_MK)),
                  pl.BlockSpec((L, GROUP), lambda c: (c, P_MV)),
                  pl.BlockSpec((L, GROUP), lambda c: (c, P_MO)),
                  pl.BlockSpec((L, LANES), lambda c: (c, 0)),
                  pl.BlockSpec((LANES, L), lambda c: (0, c)),
                  pl.BlockSpec((1, LANES), lambda c: (0, 0)),
                  pl.BlockSpec((LANES, LANES), lambda c: (0, 0)),
                  pl.BlockSpec((1, GROUP), lambda c: (0, 0))],
        out_specs=pl.BlockSpec((L, GROUP), lambda c: (c, 0)),
        out_shape=jax.ShapeDtypeStruct((S, GROUP), BF16),
        scratch_shapes=[pltpu.VMEM((HEADS, HEAD_DIM, 2 * HEAD_DIM), F32), pltpu.VMEM((8, LANES), F32)],
        compiler_params=_params("arbitrary"),
        name="mlstm",
    )(pb, pb, pb, pf, gates, gates_t, bias_row, bias_col, head_gain)


def _softplus(z):
    return jnp.maximum(z, 0.0) + jnp.log(1.0 + jnp.exp(-jnp.abs(z)))


def _sb_kernel(q_ref, k_ref, v_ref, u_ref, y_ref, acc_ref, carry_ref, *, tq, tk):
    i = pl.program_id(1)
    q = q_ref[...]
    acc_ref[...] = jnp.zeros_like(acc_ref)
    carry_ref[...] = jnp.zeros_like(carry_ref)
    q_pos = i * tq + lax.broadcasted_iota(jnp.int32, (tq, tk), 0)
    k_off = lax.broadcasted_iota(jnp.int32, (tq, tk), 1)

    def block(j, masked):
        rows = pl.ds(pl.multiple_of(j * tk, tk), tk)
        kb = k_ref[rows, :]
        vb = v_ref[rows, :]
        z = lax.dot_general(q, kb, (((1,), (1,)), ((), ())), preferred_element_type=F32)
        sp = _softplus(z)
        if masked:
            past = (j * tk + k_off) < q_pos
            sp = jnp.where(past, sp, 0.0)
        hi = sp.astype(BF16)
        lo = (sp - hi.astype(F32)).astype(BF16)
        R = (jnp.dot(hi, u_ref[...], preferred_element_type=F32)
             + jnp.dot(lo, u_ref[...], preferred_element_type=F32))
        carry = carry_ref[...]
        A = jnp.exp(z - R - carry)
        if masked:
            A = jnp.where(past, A, 0.0)
        acc_ref[...] += jnp.dot(A.astype(BF16), vb, preferred_element_type=F32)
        carry_ref[...] = carry + R[:, 0:1]

    n_diag = tq // tk
    j_hi = (i + 1) * n_diag
    for d in range(n_diag):
        block(j_hi - 1 - d, True)

    def cond(state):
        jj, live = state
        return jnp.logical_and(jj < j_hi - n_diag, live)

    def body(state):
        jj, _ = state
        block(j_hi - n_diag - 1 - jj, False)
        return jj + 1, jnp.min(carry_ref[...]) < SB_DEAD_CARRY

    lax.while_loop(cond, body, (jnp.int32(0), jnp.min(carry_ref[...]) < SB_DEAD_CARRY))
    y_ref[...] = acc_ref[...].astype(BF16)


def _sb_attention(pb, tri_u, *, tq=512, tk=256):
    S = pb.shape[0]
    tq = min(tq, S)
    tk = min(tk, tq)
    dh = HEAD_DIM
    qcol, kcol, vcol = P_SQ * HEADS, P_SK * HEADS, P_SV * HEADS
    return pl.pallas_call(
        functools.partial(_sb_kernel, tq=tq, tk=tk),
        grid=(HEADS, S // tq),
        in_specs=[pl.BlockSpec((tq, dh), lambda h, i: (i, qcol + h)),
                  pl.BlockSpec((S, dh), lambda h, i: (0, kcol + h)),
                  pl.BlockSpec((S, dh), lambda h, i: (0, vcol + h)),
                  pl.BlockSpec((tk, tk), lambda h, i: (0, 0))],
        out_specs=pl.BlockSpec((tq, dh), lambda h, i: (i, h)),
        out_shape=jax.ShapeDtypeStruct((S, GROUP), BF16),
        scratch_shapes=[pltpu.VMEM((tq, dh), F32), pltpu.VMEM((tq, 1), F32)],
        compiler_params=_params("arbitrary", "arbitrary"),
        name="sb_attention",
    )(pb, pb, pb, tri_u)


def _pack_w_in_kernel(w_ref, main_ref, gate_ref):
    G = GROUP
    gate0 = 7 * G
    after = gate0 + 2 * HEADS
    starts = (0, G, 2 * G, 6 * G, after,
              3 * G, 4 * G, 5 * G,
              after + G, after + 2 * G, after + 3 * G)
    for p, s in enumerate(starts):
        main_ref[:, p * G:(p + 1) * G] = w_ref[:, s:s + G].astype(BF16)
    lane = lax.broadcasted_iota(jnp.int32, gate_ref.shape, 1)
    gate_ref[...] = jnp.where(lane < 2 * HEADS, w_ref[:, gate0:gate0 + LANES], 0.0).astype(BF16)


def _pack_w_in(w_in, *, tr=256):
    depth, D, N = w_in.shape
    n_main = (F32_PANELS + BF16_PANELS) * GROUP
    main, gate = pl.pallas_call(
        _pack_w_in_kernel,
        grid=(depth, D // tr),
        in_specs=[pl.BlockSpec((None, tr, N), lambda l, r: (l, r, 0))],
        out_specs=[pl.BlockSpec((None, tr, n_main), lambda l, r: (l, r, 0)),
                   pl.BlockSpec((None, tr, LANES), lambda l, r: (l, r, 0))],
        out_shape=[jax.ShapeDtypeStruct((depth, D, n_main), BF16),
                   jax.ShapeDtypeStruct((depth, D, LANES), BF16)],
        compiler_params=_params("arbitrary", "arbitrary"),
        name="pack_w_in",
    )(w_in)
    return main, gate, jnp.swapaxes(gate, 1, 2)


def _token_mix(h, x, layer, w_main, w_gate, w_gate_t, w_out, conv_w, pool_w, pool_scale, i_bias, f_bias,
               head_gain, g_post, g_next, tri_u):
    pf, pb, gates, gates_t = _in_proj(h, w_main, w_gate, w_gate_t, layer)
    y_conv, y_pool = _conv_pool(pf, conv_w, pool_w, pool_scale[None, :])
    bias = jnp.pad(jnp.concatenate([i_bias, f_bias]), (0, LANES - 2 * HEADS))
    bias_col = jnp.broadcast_to(bias[:, None], (LANES, LANES))
    y_mlstm = _mlstm(pf, pb, gates, gates_t, bias[None, :], bias_col, head_gain[None, :])
    y_sb = _sb_attention(pb, tri_u)
    return _proj_res([y_conv, y_mlstm, y_pool, y_sb], w_out, layer, x, g_post, g_next, res_scale=1.0)


def kernel(x, w_in, w_out, conv_w, pool_w, pool_scale, mlstm_i_bias, mlstm_f_bias, mlstm_head_gain,
           ffn1_w_gate, ffn1_w_up, ffn1_w_down, ffn2_w_gate, ffn2_w_up, ffn2_w_down, norm_gains):
    B, S, D = x.shape
    depth = w_in.shape[0]
    outs = []
    tk = min(256, S)
    tri_u = (jnp.arange(tk)[:, None] >= jnp.arange(tk)[None, :]).astype(BF16)
    w_main, w_gate, w_gate_t = _pack_w_in(w_in)
    w_out_bf = w_out.astype(BF16)
    wd1_bf = ffn1_w_down.astype(BF16)
    wd2_bf = ffn2_w_down.astype(BF16)
    for b in range(B):
        xb = x[b]
        h = _first_norm(xb, norm_gains[0, 0][None, :])
        for l in range(depth):
            g = norm_gains[l]
            g_after = norm_gains[l + 1, 0] if l + 1 < depth else g[0]
            act = _ffn_up(h, ffn1_w_gate, ffn1_w_up, l)
            xb, h = _proj_res([act], wd1_bf, l, xb, g[1][None, :], g[2][None, :], res_scale=0.5, tm=1024, tk=512)
            xb, h = _token_mix(h, xb, l, w_main, w_gate, w_gate_t, w_out_bf, conv_w[l], pool_w[l],
                               pool_scale[l], mlstm_i_bias[l], mlstm_f_bias[l], mlstm_head_gain[l],
                               g[3][None, :], g[4][None, :], tri_u)
            act = _ffn_up(h, ffn2_w_gate, ffn2_w_up, l)
            xb, h = _proj_res([act], wd2_bf, l, xb, g[5][None, :], g_after[None, :], res_scale=0.5, tm=1024, tk=512)
        outs.append(xb)
    return jnp.stack(outs, axis=0)
```

```python
import functools

import jax
import jax.numpy as jnp
from jax import lax
from jax.experimental import pallas as pl
from jax.experimental.pallas import tpu as pltpu

F32 = jnp.float32
BF16 = jnp.bfloat16

NORM_EPS = 1e-6
HEAD_DIM = 128
HEADS = 4
GROUP = HEADS * HEAD_DIM
CONV_WIDTH = 3
POOL_WINDOWS = (2, 4, 8, 16)
POOL_HALO = 16
MLSTM_CHUNK = 128
LANES = 128
NEG_BIG = -1e30
SB_DEAD_CARRY = 105.0

VMEM_LIMIT = 56 * 1024 * 1024

F32_PANELS = 5
BF16_PANELS = 6
P_CB, P_CC, P_CU, P_MO, P_PU = range(F32_PANELS)
P_MQ, P_MK, P_MV, P_SQ, P_SK, P_SV = range(BF16_PANELS)


def _params(*sem):
    return pltpu.CompilerParams(dimension_semantics=sem, vmem_limit_bytes=VMEM_LIMIT)


def _rms_rows(y, g):
    ms = jnp.mean(y * y, axis=-1, keepdims=True)
    return y * lax.rsqrt(ms + NORM_EPS) * g


def _split3(x):
    hi = x.astype(BF16)
    r1 = x - hi.astype(F32)
    mid = r1.astype(BF16)
    lo = (r1 - mid.astype(F32)).astype(BF16)
    return hi, mid, lo


def _norm_kernel(x_ref, g_ref, h_ref):
    h_ref[...] = _rms_rows(x_ref[...], g_ref[...]).astype(BF16)


def _first_norm(x, g):
    S, D = x.shape
    tm = 256
    return pl.pallas_call(
        _norm_kernel,
        grid=(S // tm,),
        in_specs=[pl.BlockSpec((tm, D), lambda i: (i, 0)),
                  pl.BlockSpec((1, D), lambda i: (0, 0))],
        out_specs=pl.BlockSpec((tm, D), lambda i: (i, 0)),
        out_shape=jax.ShapeDtypeStruct((S, D), BF16),
        compiler_params=_params("arbitrary"),
        name="first_norm",
    )(x, g)


def _ffn_up_kernel(h_ref, wg_ref, wu_ref, o_ref, wg_bf, wu_bf, *, row_chunk):
    @pl.when(pl.program_id(1) == 0)
    def _():
        wg_bf[...] = wg_ref[...].astype(BF16)
        wu_bf[...] = wu_ref[...].astype(BF16)

    def body(r, carry):
        rows = pl.ds(pl.multiple_of(r * row_chunk, row_chunk), row_chunk)
        h = h_ref[rows, :]
        g = jnp.dot(h, wg_bf[...], preferred_element_type=F32)
        u = jnp.dot(h, wu_bf[...], preferred_element_type=F32)
        o_ref[rows, :] = (g / (1.0 + jnp.exp(-g)) * u).astype(BF16)
        return carry

    lax.fori_loop(0, h_ref.shape[0] // row_chunk, body, 0)


def _ffn_up(h, wg, wu, layer, *, tm=2048, tn=512, row_chunk=512):
    S, D = h.shape
    N = wg.shape[2]
    tm = min(tm, S)
    return pl.pallas_call(
        functools.partial(_ffn_up_kernel, row_chunk=min(row_chunk, tm)),
        grid=(N // tn, S // tm),
        in_specs=[pl.BlockSpec((tm, D), lambda j, i: (i, 0)),
                  pl.BlockSpec((None, D, tn), lambda j, i: (layer, 0, j)),
                  pl.BlockSpec((None, D, tn), lambda j, i: (layer, 0, j))],
        out_specs=pl.BlockSpec((tm, tn), lambda j, i: (i, j)),
        out_shape=jax.ShapeDtypeStruct((S, N), BF16),
        scratch_shapes=[pltpu.VMEM((D, tn), BF16), pltpu.VMEM((D, tn), BF16)],
        compiler_params=_params("arbitrary", "arbitrary"),
        name="ffn_up",
    )(h, wg, wu)


def _proj_res_kernel(*refs, n_lhs, nk, res_scale, row_chunk, ep_chunk):
    a_refs = refs[:n_lhs]
    w_ref, x_ref, gp_ref, gn_ref, xo_ref, ho_ref = refs[n_lhs:]
    k = pl.program_id(1)
    tm = xo_ref.shape[0]
    kw = a_refs[0].shape[1]

    def accumulate(first):
        def body(r, carry):
            rows = pl.ds(pl.multiple_of(r * row_chunk, row_chunk), row_chunk)
            part = jnp.dot(a_refs[0][rows, :], w_ref[0:kw, :], preferred_element_type=F32)
            for g in range(1, n_lhs):
                part = part + jnp.dot(a_refs[g][rows, :], w_ref[g * kw:(g + 1) * kw, :],
                                      preferred_element_type=F32)
            if first:
                xo_ref[rows, :] = part
            else:
                xo_ref[rows, :] += part
            return carry
        lax.fori_loop(0, tm // row_chunk, body, 0)

    def epilogue():
        def body(r, carry):
            rows = pl.ds(pl.multiple_of(r * ep_chunk, ep_chunk), ep_chunk)
            xn = x_ref[rows, :] + res_scale * _rms_rows(xo_ref[rows, :], gp_ref[...])
            xo_ref[rows, :] = xn
            ho_ref[rows, :] = _rms_rows(xn, gn_ref[...]).astype(BF16)
            return carry
        lax.fori_loop(0, tm // ep_chunk, body, 0, unroll=2)

    if nk == 1:
        accumulate(True)
        epilogue()
    else:
        pl.when(k == 0)(lambda: accumulate(True))
        pl.when(k > 0)(lambda: accumulate(False))
        pl.when(k == nk - 1)(epilogue)


def _proj_res(lhs, w, layer, x, g_post, g_next, *, res_scale, tm=512, tk=1408, row_chunk=512, ep_chunk=64):
    S, D = x.shape
    tm = min(tm, S)
    n_lhs = len(lhs)
    kw = lhs[0].shape[1]
    if n_lhs > 1:
        tk = n_lhs * kw
    nk = w.shape[1] // tk
    return pl.pallas_call(
        functools.partial(_proj_res_kernel, n_lhs=n_lhs, nk=nk, res_scale=res_scale,
                          row_chunk=min(row_chunk, tm), ep_chunk=ep_chunk),
        grid=(S // tm, nk),
        in_specs=[pl.BlockSpec((tm, tk // n_lhs), lambda i, k: (i, k)) for _ in lhs] + [
            pl.BlockSpec((None, tk, D), lambda i, k: (layer, k, 0)),
            pl.BlockSpec((tm, D), lambda i, k: (i, 0)),
            pl.BlockSpec((1, D), lambda i, k: (0, 0)),
            pl.BlockSpec((1, D), lambda i, k: (0, 0))],
        out_specs=[pl.BlockSpec((tm, D), lambda i, k: (i, 0)),
                   pl.BlockSpec((tm, D), lambda i, k: (i, 0))],
        out_shape=[jax.ShapeDtypeStruct((S, D), F32), jax.ShapeDtypeStruct((S, D), BF16)],
        compiler_params=_params("arbitrary", "arbitrary"),
        name="proj_res",
    )(*lhs, w, x, g_post, g_next)


def _in_proj_kernel(h_ref, w_ref, wg_ref, wgt_ref, of_ref, ob_ref, g_ref, gt_ref, *, scale):
    h = h_ref[...]
    for p in range(F32_PANELS):
        cols = slice(p * GROUP, (p + 1) * GROUP)
        of_ref[:, cols] = jnp.dot(h, w_ref[:, cols], preferred_element_type=F32)
    for p in range(BF16_PANELS):
        cols = slice(p * GROUP, (p + 1) * GROUP)
        wcols = slice((F32_PANELS + p) * GROUP, (F32_PANELS + p + 1) * GROUP)
        r = jnp.dot(h, w_ref[:, wcols], preferred_element_type=F32)
        if p in (P_MK, P_SQ):
            r = r * scale
        ob_ref[:, cols] = r.astype(BF16)
    g_ref[...] = jnp.dot(h, wg_ref[...], preferred_element_type=F32)
    gt_ref[...] = lax.dot_general(wgt_ref[...], h, (((1,), (1,)), ((), ())), preferred_element_type=F32)


def _in_proj(h, w_main, w_gate, w_gate_t, layer, *, tm=512):
    S, D = h.shape
    tm = min(tm, S)
    nf, nb = F32_PANELS * GROUP, BF16_PANELS * GROUP
    return pl.pallas_call(
        functools.partial(_in_proj_kernel, scale=HEAD_DIM ** -0.5),
        grid=(S // tm,),
        in_specs=[pl.BlockSpec((tm, D), lambda i: (i, 0)),
                  pl.BlockSpec((None, D, nf + nb), lambda i: (layer, 0, 0), pipeline_mode=pl.Buffered(1)),
                  pl.BlockSpec((None, D, LANES), lambda i: (layer, 0, 0)),
                  pl.BlockSpec((None, LANES, D), lambda i: (layer, 0, 0))],
        out_specs=[pl.BlockSpec((tm, nf), lambda i: (i, 0)),
                   pl.BlockSpec((tm, nb), lambda i: (i, 0)),
                   pl.BlockSpec((tm, LANES), lambda i: (i, 0)),
                   pl.BlockSpec((LANES, tm), lambda i: (0, i))],
        out_shape=[jax.ShapeDtypeStruct((S, nf), F32),
                   jax.ShapeDtypeStruct((S, nb), BF16),
                   jax.ShapeDtypeStruct((S, LANES), F32),
                   jax.ShapeDtypeStruct((LANES, S), F32)],
        compiler_params=_params("arbitrary"),
        name="in_proj",
    )(h, w_main, w_gate, w_gate_t)


def _conv_pool_kernel(cb_ref, cc_ref, cu_ref, pu_ref, cch_ref, cuh_ref, puh_ref,
                      cw_ref, pw_ref, ps_ref, yc_ref, yp_ref, zbuf, pbuf):
    i = pl.program_id(0)
    tm = cb_ref.shape[0]
    keep = (i > 0).astype(F32)
    H = POOL_HALO

    zbuf[0:H, :] = cch_ref[...] * cuh_ref[...] * keep
    zbuf[H:H + tm, :] = cc_ref[...] * cu_ref[...]
    y = cw_ref[0:1, :] * zbuf[H - 2:H - 2 + tm, :]
    y = y + cw_ref[1:2, :] * zbuf[H - 1:H - 1 + tm, :]
    y = y + cw_ref[2:3, :] * zbuf[H:H + tm, :]
    yc_ref[...] = (cb_ref[...] * y).astype(BF16)

    pbuf[0:H, :] = puh_ref[...] * keep
    pbuf[H:H + tm, :] = pu_ref[...]
    t = i * tm + lax.broadcasted_iota(jnp.int32, (tm, 1), 0)
    for g, win in enumerate(POOL_WINDOWS):
        cols = slice(g * LANES, (g + 1) * LANES)
        u = pbuf[H:H + tm, cols]
        wsum = u
        for d in range(1, win):
            wsum = wsum + pbuf[H - d:H - d + tm, cols]
        count = jnp.minimum(t + 1, win).astype(F32)
        pooled = wsum / count - u
        yg = jnp.dot(pooled.astype(BF16), pw_ref[g].astype(BF16), preferred_element_type=F32)
        yp_ref[:, cols] = (yg * ps_ref[:, cols]).astype(BF16)


def _conv_pool(pf, conv_w, pool_w, pool_scale, *, tm=512):
    S = pf.shape[0]
    tm = min(tm, S)
    H = POOL_HALO
    hb = tm // H

    def cur(p):
        return pl.BlockSpec((tm, GROUP), lambda i, p=p: (i, p))

    def halo(p):
        return pl.BlockSpec((H, GROUP), lambda i, p=p: (jnp.maximum(i * hb - 1, 0), p))

    return pl.pallas_call(
        _conv_pool_kernel,
        grid=(S // tm,),
        in_specs=[cur(P_CB), cur(P_CC), cur(P_CU), cur(P_PU), halo(P_CC), halo(P_CU), halo(P_PU),
                  pl.BlockSpec((CONV_WIDTH, GROUP), lambda i: (0, 0)),
                  pl.BlockSpec(pool_w.shape, lambda i: (0, 0, 0)),
                  pl.BlockSpec((1, GROUP), lambda i: (0, 0))],
        out_specs=[pl.BlockSpec((tm, GROUP), lambda i: (i, 0)),
                   pl.BlockSpec((tm, GROUP), lambda i: (i, 0))],
        out_shape=[jax.ShapeDtypeStruct((S, GROUP), BF16), jax.ShapeDtypeStruct((S, GROUP), BF16)],
        scratch_shapes=[pltpu.VMEM((H + tm, GROUP), F32), pltpu.VMEM((H + tm, GROUP), F32)],
        compiler_params=_params("arbitrary"),
        name="conv_pool",
    )(pf, pf, pf, pf, pf, pf, pf, conv_w, pool_w, pool_scale)


def _log_sigmoid(x):
    return jnp.minimum(x, 0.0) - jnp.log(1.0 + jnp.exp(-jnp.abs(x)))


def _mlstm_kernel(q_ref, k_ref, v_ref, o_ref, g_ref, gt_ref, brow_ref, bcol_ref, gain_ref,
                  y_ref, c_ref, m_ref, *, n_sub):
    L = MLSTM_CHUNK
    dh = HEAD_DIM

    @pl.when(pl.program_id(0) == 0)
    def _():
        c_ref[...] = jnp.zeros_like(c_ref)
        m_ref[...] = jnp.zeros_like(m_ref)

    row = lax.broadcasted_iota(jnp.int32, (L, L), 0)
    col = lax.broadcasted_iota(jnp.int32, (L, L), 1)
    causal = col <= row
    tri = causal.astype(BF16)
    tri_t = (row <= col).astype(BF16)
    ones_col = (lax.broadcasted_iota(jnp.int32, (L, dh), 1) == 0).astype(BF16)

    for sub in range(n_sub):
        rows = slice(sub * L, (sub + 1) * L)
        G = g_ref[rows, :] + brow_ref[...]
        GT = gt_ref[:, rows] + bcol_ref[...]
        bc = sum(jnp.dot(tri, part, preferred_element_type=F32) for part in _split3(_log_sigmoid(G)))
        br = sum(jnp.dot(part, tri_t, preferred_element_type=F32) for part in _split3(_log_sigmoid(GT)))

        y_new, c_new, m_new_rows = [], [], []
        for h in range(HEADS):
            cols = slice(h * dh, (h + 1) * dh)
            q = q_ref[rows, cols]
            k = k_ref[rows, cols]
            v = v_ref[rows, cols]
            b_col = bc[:, HEADS + h:HEADS + h + 1]
            b_row = br[HEADS + h:HEADS + h + 1, :]
            ig_col = G[:, h:h + 1]
            ig_row = GT[h:h + 1, :]
            m_prev = m_ref[h:h + 1, 0:1]
            c_ext = c_ref[h]

            D = jnp.where(causal, b_col - b_row + ig_row, NEG_BIG)
            inter = b_col + m_prev
            m_t = jnp.maximum(inter, jnp.max(D, axis=1, keepdims=True))
            w_intra = jnp.exp(D - m_t)
            w_inter = jnp.exp(inter - m_t)
            qk = lax.dot_general(q, k, (((1,), (1,)), ((), ())), preferred_element_type=F32)
            s = qk * w_intra
            qc = jnp.dot(q, c_ext.astype(BF16), preferred_element_type=F32)
            num = jnp.dot(s.astype(BF16), v, preferred_element_type=F32) + w_inter * qc[:, :dh]
            den = jnp.sum(s, axis=1, keepdims=True) + w_inter * qc[:, dh:dh + 1]
            hh = num / jnp.maximum(jnp.abs(den), jnp.exp(-m_t))
            hh = hh * lax.rsqrt(jnp.mean(hh * hh, axis=1, keepdims=True) + NORM_EPS)
            o = o_ref[rows, cols]
            y_new.append((hh * gain_ref[:, cols] / (1.0 + jnp.exp(-o))).astype(BF16))

            b_last = b_col[L - 1:L, :]
            g_col = b_last - b_col + ig_col
            m_new = jnp.maximum(b_last + m_prev, jnp.max(g_col, axis=0, keepdims=True))
            decay = jnp.exp(b_last + m_prev - m_new)
            wk = jnp.exp(g_col - m_new)
            kw_t = (wk * k.astype(F32)).T.astype(BF16)
            v_ext = jnp.concatenate([v, ones_col], axis=1)
            c_new.append(decay * c_ext + jnp.dot(kw_t, v_ext, preferred_element_type=F32))
            m_new_rows.append(jnp.broadcast_to(m_new, (1, LANES)))

        y_ref[rows, :] = jnp.concatenate(y_new, axis=1)
        for h in range(HEADS):
            c_ref[h] = c_new[h]
            m_ref[h:h + 1, :] = m_new_rows[h]


def _mlstm(pf, pb, gates, gates_t, bias_row, bias_col, head_gain, *, n_sub=1):
    S = pf.shape[0]
    L = n_sub * MLSTM_CHUNK
    return pl.pallas_call(
        functools.partial(_mlstm_kernel, n_sub=n_sub),
        grid=(S // L,),
        in_specs=[pl.BlockSpec((L, GROUP), lambda c: (c, P_MQ)),
                  pl.BlockSpec((L, GROUP), lambda c: (c, P_MK)),
                  pl.BlockSpec((L, GROUP), lambda c: (c, P_MV)),
                  pl.BlockSpec((L, GROUP), lambda c: (c, P_MO)),
                  pl.BlockSpec((L, LANES), lambda c: (c, 0)),
                  pl.BlockSpec((LANES, L), lambda c: (0, c)),
                  pl.BlockSpec((1, LANES), lambda c: (0, 0)),
                  pl.BlockSpec((LANES, LANES), lambda c: (0, 0)),
                  pl.BlockSpec((1, GROUP), lambda c: (0, 0))],
        out_specs=pl.BlockSpec((L, GROUP), lambda c: (c, 0)),
        out_shape=jax.ShapeDtypeStruct((S, GROUP), BF16),
        scratch_shapes=[pltpu.VMEM((HEADS, HEAD_DIM, 2 * HEAD_DIM), F32), pltpu.VMEM((8, LANES), F32)],
        compiler_params=_params("arbitrary"),
        name="mlstm",
    )(pb, pb, pb, pf, gates, gates_t, bias_row, bias_col, head_gain)


def _softplus(z):
    return jnp.maximum(z, 0.0) + jnp.log(1.0 + jnp.exp(-jnp.abs(z)))


def _sb_kernel(q_ref, k_ref, v_ref, u_ref, y_ref, acc_ref, carry_ref, *, tq, tk):
    i = pl.program_id(1)
    q = q_ref[...]
    acc_ref[...] = jnp.zeros_like(acc_ref)
    carry_ref[...] = jnp.zeros_like(carry_ref)
    q_pos = i * tq + lax.broadcasted_iota(jnp.int32, (tq, tk), 0)
    k_off = lax.broadcasted_iota(jnp.int32, (tq, tk), 1)

    def block(j, masked):
        rows = pl.ds(pl.multiple_of(j * tk, tk), tk)
        kb = k_ref[rows, :]
        vb = v_ref[rows, :]
        z = lax.dot_general(q, kb, (((1,), (1,)), ((), ())), preferred_element_type=F32)
        sp = _softplus(z)
        if masked:
            past = (j * tk + k_off) < q_pos
            sp = jnp.where(past, sp, 0.0)
        hi = sp.astype(BF16)
        lo = (sp - hi.astype(F32)).astype(BF16)
        R = (jnp.dot(hi, u_ref[...], preferred_element_type=F32)
             + jnp.dot(lo, u_ref[...], preferred_element_type=F32))
        carry = carry_ref[...]
        A = jnp.exp(z - R - carry)
        if masked:
            A = jnp.where(past, A, 0.0)
        acc_ref[...] += jnp.dot(A.astype(BF16), vb, preferred_element_type=F32)
        carry_ref[...] = carry + R[:, 0:1]

    n_diag = tq // tk
    j_hi = (i + 1) * n_diag
    for d in range(n_diag):
        block(j_hi - 1 - d, True)

    def cond(state):
        jj, live = state
        return jnp.logical_and(jj < j_hi - n_diag, live)

    def body(state):
        jj, _ = state
        block(j_hi - n_diag - 1 - jj, False)
        return jj + 1, jnp.min(carry_ref[...]) < SB_DEAD_CARRY

    lax.while_loop(cond, body, (jnp.int32(0), jnp.min(carry_ref[...]) < SB_DEAD_CARRY))
    y_ref[...] = acc_ref[...].astype(BF16)


def _sb_attention(pb, tri_u, *, tq=512, tk=256):
    S = pb.shape[0]
    tq = min(tq, S)
    tk = min(tk, tq)
    dh = HEAD_DIM
    qcol, kcol, vcol = P_SQ * HEADS, P_SK * HEADS, P_SV * HEADS
    return pl.pallas_call(
        functools.partial(_sb_kernel, tq=tq, tk=tk),
        grid=(HEADS, S // tq),
        in_specs=[pl.BlockSpec((tq, dh), lambda h, i: (i, qcol + h)),
                  pl.BlockSpec((S, dh), lambda h, i: (0, kcol + h)),
                  pl.BlockSpec((S, dh), lambda h, i: (0, vcol + h)),
                  pl.BlockSpec((tk, tk), lambda h, i: (0, 0))],
        out_specs=pl.BlockSpec((tq, dh), lambda h, i: (i, h)),
        out_shape=jax.ShapeDtypeStruct((S, GROUP), BF16),
        scratch_shapes=[pltpu.VMEM((tq, dh), F32), pltpu.VMEM((tq, 1), F32)],
        compiler_params=_params("arbitrary", "arbitrary"),
        name="sb_attention",
    )(pb, pb, pb, tri_u)


def _pack_w_in_kernel(wt_ref, main_ref):
    main_ref[...] = wt_ref[0].T.astype(BF16)


def _pack_w_in(w_in):
    depth, D, N = w_in.shape
    G = GROUP
    gate0 = 7 * G
    after = gate0 + 2 * HEADS
    starts = (0, G, 2 * G, 6 * G, after,
              3 * G, 4 * G, 5 * G,
              after + G, after + 2 * G, after + 3 * G)
    w_t = jnp.swapaxes(w_in, 1, 2)

    def panel_start(p):
        s = jnp.int32(starts[0] // 8)
        for idx in range(1, len(starts)):
            s = jnp.where(p == idx, starts[idx] // 8, s)
        return s * 8

    main = pl.pallas_call(
        _pack_w_in_kernel,
        grid=(depth, len(starts)),
        in_specs=[pl.BlockSpec((pl.Element(1), pl.Element(G), pl.Element(D)),
                               lambda l, p: (l, panel_start(p), 0))],
        out_specs=pl.BlockSpec((None, D, G), lambda l, p: (l, 0, p)),
        out_shape=jax.ShapeDtypeStruct((depth, D, len(starts) * G), BF16),
        compiler_params=_params("arbitrary", "arbitrary"),
        name="pack_w_in",
    )(w_t)
    gate_t = jnp.pad(w_t[:, gate0:after, :], ((0, 0), (0, LANES - 2 * HEADS), (0, 0))).astype(BF16)
    return main, jnp.swapaxes(gate_t, 1, 2), gate_t


def _token_mix(h, x, layer, w_main, w_gate, w_gate_t, w_out, conv_w, pool_w, pool_scale, i_bias, f_bias,
               head_gain, g_post, g_next, tri_u):
    pf, pb, gates, gates_t = _in_proj(h, w_main, w_gate, w_gate_t, layer)
    y_conv, y_pool = _conv_pool(pf, conv_w, pool_w, pool_scale[None, :])
    bias = jnp.pad(jnp.concatenate([i_bias, f_bias]), (0, LANES - 2 * HEADS))
    bias_col = jnp.broadcast_to(bias[:, None], (LANES, LANES))
    y_mlstm = _mlstm(pf, pb, gates, gates_t, bias[None, :], bias_col, head_gain[None, :])
    y_sb = _sb_attention(pb, tri_u)
    return _proj_res([y_conv, y_mlstm, y_pool, y_sb], w_out, layer, x, g_post, g_next, res_scale=1.0)


def kernel(x, w_in, w_out, conv_w, pool_w, pool_scale, mlstm_i_bias, mlstm_f_bias, mlstm_head_gain,
           ffn1_w_gate, ffn1_w_up, ffn1_w_down, ffn2_w_gate, ffn2_w_up, ffn2_w_down, norm_gains):
    B, S, D = x.shape
    depth = w_in.shape[0]
    outs = []
    tk = min(256, S)
    tri_u = (jnp.arange(tk)[:, None] >= jnp.arange(tk)[None, :]).astype(BF16)
    w_main, w_gate, w_gate_t = _pack_w_in(w_in)
    w_out_bf = w_out.astype(BF16)
    wd1_bf = ffn1_w_down.astype(BF16)
    wd2_bf = ffn2_w_down.astype(BF16)
    for b in range(B):
        xb = x[b]
        h = _first_norm(xb, norm_gains[0, 0][None, :])
        for l in range(depth):
            g = norm_gains[l]
            g_after = norm_gains[l + 1, 0] if l + 1 < depth else g[0]
            act = _ffn_up(h, ffn1_w_gate, ffn1_w_up, l)
            xb, h = _proj_res([act], wd1_bf, l, xb, g[1][None, :], g[2][None, :], res_scale=0.5)
            xb, h = _token_mix(h, xb, l, w_main, w_gate, w_gate_t, w_out_bf, conv_w[l], pool_w[l],
                               pool_scale[l], mlstm_i_bias[l], mlstm_f_bias[l], mlstm_head_gain[l],
                               g[3][None, :], g[4][None, :], tri_u)
            act = _ffn_up(h, ffn2_w_gate, ffn2_w_up, l)
            xb, h = _proj_res([act], wd2_bf, l, xb, g[5][None, :], g_after[None, :], res_scale=0.5)
        outs.append(xb)
    return jnp.stack(outs, axis=0)
```

```python
import functools

import jax
import jax.numpy as jnp
from jax import lax
from jax.experimental import pallas as pl
from jax.experimental.pallas import tpu as pltpu

F32 = jnp.float32
BF16 = jnp.bfloat16

NORM_EPS = 1e-6
HEAD_DIM = 128
HEADS = 4
GROUP = HEADS * HEAD_DIM
CONV_WIDTH = 3
POOL_WINDOWS = (2, 4, 8, 16)
POOL_HALO = 16
MLSTM_CHUNK = 128
LANES = 128
NEG_BIG = -1e30
SB_DEAD_CARRY = 105.0

VMEM_LIMIT = 56 * 1024 * 1024

F32_PANELS = 5
BF16_PANELS = 6
P_CB, P_CC, P_CU, P_MO, P_PU = range(F32_PANELS)
P_MQ, P_MK, P_MV, P_SQ, P_SK, P_SV = range(BF16_PANELS)


def _params(*sem):
    return pltpu.CompilerParams(dimension_semantics=sem, vmem_limit_bytes=VMEM_LIMIT)


def _rms_rows(y, g):
    ms = jnp.mean(y * y, axis=-1, keepdims=True)
    return y * lax.rsqrt(ms + NORM_EPS) * g


def _split3(x):
    hi = x.astype(BF16)
    r1 = x - hi.astype(F32)
    mid = r1.astype(BF16)
    lo = (r1 - mid.astype(F32)).astype(BF16)
    return hi, mid, lo


def _norm_kernel(x_ref, g_ref, h_ref):
    h_ref[...] = _rms_rows(x_ref[...], g_ref[...]).astype(BF16)


def _first_norm(x, g):
    S, D = x.shape
    tm = 256
    return pl.pallas_call(
        _norm_kernel,
        grid=(S // tm,),
        in_specs=[pl.BlockSpec((tm, D), lambda i: (i, 0)),
                  pl.BlockSpec((1, D), lambda i: (0, 0))],
        out_specs=pl.BlockSpec((tm, D), lambda i: (i, 0)),
        out_shape=jax.ShapeDtypeStruct((S, D), BF16),
        compiler_params=_params("arbitrary"),
        name="first_norm",
    )(x, g)


def _ffn_up_kernel(h_ref, wg_ref, wu_ref, o_ref, wg_bf, wu_bf, *, row_chunk):
    @pl.when(pl.program_id(1) == 0)
    def _():
        wg_bf[...] = wg_ref[...].astype(BF16)
        wu_bf[...] = wu_ref[...].astype(BF16)

    def body(r, carry):
        rows = pl.ds(pl.multiple_of(r * row_chunk, row_chunk), row_chunk)
        h = h_ref[rows, :]
        g = jnp.dot(h, wg_bf[...], preferred_element_type=F32)
        u = jnp.dot(h, wu_bf[...], preferred_element_type=F32)
        o_ref[rows, :] = (g / (1.0 + jnp.exp(-g)) * u).astype(BF16)
        return carry

    lax.fori_loop(0, h_ref.shape[0] // row_chunk, body, 0)


def _ffn_up(h, wg, wu, layer, *, tm=2048, tn=512, row_chunk=1024):
    S, D = h.shape
    N = wg.shape[2]
    tm = min(tm, S)
    return pl.pallas_call(
        functools.partial(_ffn_up_kernel, row_chunk=min(row_chunk, tm)),
        grid=(N // tn, S // tm),
        in_specs=[pl.BlockSpec((tm, D), lambda j, i: (i, 0)),
                  pl.BlockSpec((None, D, tn), lambda j, i: (layer, 0, j)),
                  pl.BlockSpec((None, D, tn), lambda j, i: (layer, 0, j))],
        out_specs=pl.BlockSpec((tm, tn), lambda j, i: (i, j)),
        out_shape=jax.ShapeDtypeStruct((S, N), BF16),
        scratch_shapes=[pltpu.VMEM((D, tn), BF16), pltpu.VMEM((D, tn), BF16)],
        compiler_params=_params("arbitrary", "arbitrary"),
        name="ffn_up",
    )(h, wg, wu)


def _proj_res_kernel(*refs, n_lhs, nk, res_scale, row_chunk, ep_chunk):
    a_refs = refs[:n_lhs]
    w_ref, x_ref, gp_ref, gn_ref, xo_ref, ho_ref = refs[n_lhs:]
    k = pl.program_id(1)
    tm = xo_ref.shape[0]
    kw = a_refs[0].shape[1]

    def accumulate(first):
        def body(r, carry):
            rows = pl.ds(pl.multiple_of(r * row_chunk, row_chunk), row_chunk)
            part = jnp.dot(a_refs[0][rows, :], w_ref[0:kw, :], preferred_element_type=F32)
            for g in range(1, n_lhs):
                part = part + jnp.dot(a_refs[g][rows, :], w_ref[g * kw:(g + 1) * kw, :],
                                      preferred_element_type=F32)
            if first:
                xo_ref[rows, :] = part
            else:
                xo_ref[rows, :] += part
            return carry
        lax.fori_loop(0, tm // row_chunk, body, 0)

    def epilogue():
        def body(r, carry):
            rows = pl.ds(pl.multiple_of(r * ep_chunk, ep_chunk), ep_chunk)
            xn = x_ref[rows, :] + res_scale * _rms_rows(xo_ref[rows, :], gp_ref[...])
            xo_ref[rows, :] = xn
            ho_ref[rows, :] = _rms_rows(xn, gn_ref[...]).astype(BF16)
            return carry
        lax.fori_loop(0, tm // ep_chunk, body, 0, unroll=2)

    if nk == 1:
        accumulate(True)
        epilogue()
    else:
        pl.when(k == 0)(lambda: accumulate(True))
        pl.when(k > 0)(lambda: accumulate(False))
        pl.when(k == nk - 1)(epilogue)


def _proj_res(lhs, w, layer, x, g_post, g_next, *, res_scale, tm=512, tk=1408, row_chunk=512, ep_chunk=64):
    S, D = x.shape
    tm = min(tm, S)
    n_lhs = len(lhs)
    kw = lhs[0].shape[1]
    if n_lhs > 1:
        tk = n_lhs * kw
    nk = w.shape[1] // tk
    return pl.pallas_call(
        functools.partial(_proj_res_kernel, n_lhs=n_lhs, nk=nk, res_scale=res_scale,
                          row_chunk=min(row_chunk, tm), ep_chunk=ep_chunk),
        grid=(S // tm, nk),
        in_specs=[pl.BlockSpec((tm, tk // n_lhs), lambda i, k: (i, k)) for _ in lhs] + [
            pl.BlockSpec((None, tk, D), lambda i, k: (layer, k, 0)),
            pl.BlockSpec((tm, D), lambda i, k: (i, 0)),
            pl.BlockSpec((1, D), lambda i, k: (0, 0)),
            pl.BlockSpec((1, D), lambda i, k: (0, 0))],
        out_specs=[pl.BlockSpec((tm, D), lambda i, k: (i, 0)),
                   pl.BlockSpec((tm, D), lambda i, k: (i, 0))],
        out_shape=[jax.ShapeDtypeStruct((S, D), F32), jax.ShapeDtypeStruct((S, D), BF16)],
        compiler_params=_params("arbitrary", "arbitrary"),
        name="proj_res",
    )(*lhs, w, x, g_post, g_next)


def _in_proj_kernel(h_ref, w_ref, wg_ref, wgt_ref, of_ref, ob_ref, g_ref, gt_ref, *, scale):
    h = h_ref[...]
    for p in range(F32_PANELS):
        cols = slice(p * GROUP, (p + 1) * GROUP)
        of_ref[:, cols] = jnp.dot(h, w_ref[:, cols], preferred_element_type=F32)
    for p in range(BF16_PANELS):
        cols = slice(p * GROUP, (p + 1) * GROUP)
        wcols = slice((F32_PANELS + p) * GROUP, (F32_PANELS + p + 1) * GROUP)
        r = jnp.dot(h, w_ref[:, wcols], preferred_element_type=F32)
        if p in (P_MK, P_SQ):
            r = r * scale
        ob_ref[:, cols] = r.astype(BF16)
    g_ref[...] = jnp.dot(h, wg_ref[...], preferred_element_type=F32)
    gt_ref[...] = lax.dot_general(wgt_ref[...], h, (((1,), (1,)), ((), ())), preferred_element_type=F32)


def _in_proj(h, w_main, w_gate, w_gate_t, layer, *, tm=512):
    S, D = h.shape
    tm = min(tm, S)
    nf, nb = F32_PANELS * GROUP, BF16_PANELS * GROUP
    return pl.pallas_call(
        functools.partial(_in_proj_kernel, scale=HEAD_DIM ** -0.5),
        grid=(S // tm,),
        in_specs=[pl.BlockSpec((tm, D), lambda i: (i, 0)),
                  pl.BlockSpec((None, D, nf + nb), lambda i: (layer, 0, 0), pipeline_mode=pl.Buffered(1)),
                  pl.BlockSpec((None, D, LANES), lambda i: (layer, 0, 0)),
                  pl.BlockSpec((None, LANES, D), lambda i: (layer, 0, 0))],
        out_specs=[pl.BlockSpec((tm, nf), lambda i: (i, 0)),
                   pl.BlockSpec((tm, nb), lambda i: (i, 0)),
                   pl.BlockSpec((tm, LANES), lambda i: (i, 0)),
                   pl.BlockSpec((LANES, tm), lambda i: (0, i))],
        out_shape=[jax.ShapeDtypeStruct((S, nf), F32),
                   jax.ShapeDtypeStruct((S, nb), BF16),
                   jax.ShapeDtypeStruct((S, LANES), F32),
                   jax.ShapeDtypeStruct((LANES, S), F32)],
        compiler_params=_params("arbitrary"),
        name="in_proj",
    )(h, w_main, w_gate, w_gate_t)


def _conv_pool_kernel(cb_ref, cc_ref, cu_ref, pu_ref, cch_ref, cuh_ref, puh_ref,
                      cw_ref, pw_ref, ps_ref, yc_ref, yp_ref, zbuf, pbuf):
    i = pl.program_id(0)
    tm = cb_ref.shape[0]
    keep = (i > 0).astype(F32)
    H = POOL_HALO

    zbuf[0:H, :] = cch_ref[...] * cuh_ref[...] * keep
    zbuf[H:H + tm, :] = cc_ref[...] * cu_ref[...]
    y = cw_ref[0:1, :] * zbuf[H - 2:H - 2 + tm, :]
    y = y + cw_ref[1:2, :] * zbuf[H - 1:H - 1 + tm, :]
    y = y + cw_ref[2:3, :] * zbuf[H:H + tm, :]
    yc_ref[...] = (cb_ref[...] * y).astype(BF16)

    pbuf[0:H, :] = puh_ref[...] * keep
    pbuf[H:H + tm, :] = pu_ref[...]
    t = i * tm + lax.broadcasted_iota(jnp.int32, (tm, 1), 0)
    for g, win in enumerate(POOL_WINDOWS):
        cols = slice(g * LANES, (g + 1) * LANES)
        u = pbuf[H:H + tm, cols]
        wsum = u
        for d in range(1, win):
            wsum = wsum + pbuf[H - d:H - d + tm, cols]
        count = jnp.minimum(t + 1, win).astype(F32)
        pooled = wsum / count - u
        yg = jnp.dot(pooled.astype(BF16), pw_ref[g].astype(BF16), preferred_element_type=F32)
        yp_ref[:, cols] = (yg * ps_ref[:, cols]).astype(BF16)


def _conv_pool(pf, conv_w, pool_w, pool_scale, *, tm=512):
    S = pf.shape[0]
    tm = min(tm, S)
    H = POOL_HALO
    hb = tm // H

    def cur(p):
        return pl.BlockSpec((tm, GROUP), lambda i, p=p: (i, p))

    def halo(p):
        return pl.BlockSpec((H, GROUP), lambda i, p=p: (jnp.maximum(i * hb - 1, 0), p))

    return pl.pallas_call(
        _conv_pool_kernel,
        grid=(S // tm,),
        in_specs=[cur(P_CB), cur(P_CC), cur(P_CU), cur(P_PU), halo(P_CC), halo(P_CU), halo(P_PU),
                  pl.BlockSpec((CONV_WIDTH, GROUP), lambda i: (0, 0)),
                  pl.BlockSpec(pool_w.shape, lambda i: (0, 0, 0)),
                  pl.BlockSpec((1, GROUP), lambda i: (0, 0))],
        out_specs=[pl.BlockSpec((tm, GROUP), lambda i: (i, 0)),
                   pl.BlockSpec((tm, GROUP), lambda i: (i, 0))],
        out_shape=[jax.ShapeDtypeStruct((S, GROUP), BF16), jax.ShapeDtypeStruct((S, GROUP), BF16)],
        scratch_shapes=[pltpu.VMEM((H + tm, GROUP), F32), pltpu.VMEM((H + tm, GROUP), F32)],
        compiler_params=_params("arbitrary"),
        name="conv_pool",
    )(pf, pf, pf, pf, pf, pf, pf, conv_w, pool_w, pool_scale)


def _log_sigmoid(x):
    return jnp.minimum(x, 0.0) - jnp.log(1.0 + jnp.exp(-jnp.abs(x)))


def _mlstm_kernel(q_ref, k_ref, v_ref, o_ref, g_ref, gt_ref, brow_ref, bcol_ref, gain_ref,
                  y_ref, c_ref, m_ref, *, n_sub):
    L = MLSTM_CHUNK
    dh = HEAD_DIM

    @pl.when(pl.program_id(0) == 0)
    def _():
        c_ref[...] = jnp.zeros_like(c_ref)
        m_ref[...] = jnp.zeros_like(m_ref)

    row = lax.broadcasted_iota(jnp.int32, (L, L), 0)
    col = lax.broadcasted_iota(jnp.int32, (L, L), 1)
    causal = col <= row
    tri = causal.astype(BF16)
    tri_t = (row <= col).astype(BF16)
    ones_col = (lax.broadcasted_iota(jnp.int32, (L, dh), 1) == 0).astype(BF16)

    for sub in range(n_sub):
        rows = slice(sub * L, (sub + 1) * L)
        G = g_ref[rows, :] + brow_ref[...]
        GT = gt_ref[:, rows] + bcol_ref[...]
        bc = sum(jnp.dot(tri, part, preferred_element_type=F32) for part in _split3(_log_sigmoid(G)))
        br = sum(jnp.dot(part, tri_t, preferred_element_type=F32) for part in _split3(_log_sigmoid(GT)))

        y_new, c_new, m_new_rows = [], [], []
        for h in range(HEADS):
            cols = slice(h * dh, (h + 1) * dh)
            q = q_ref[rows, cols]
            k = k_ref[rows, cols]
            v = v_ref[rows, cols]
            b_col = bc[:, HEADS + h:HEADS + h + 1]
            b_row = br[HEADS + h:HEADS + h + 1, :]
            ig_col = G[:, h:h + 1]
            ig_row = GT[h:h + 1, :]
            m_prev = m_ref[h:h + 1, 0:1]
            c_ext = c_ref[h]

            D = jnp.where(causal, b_col - b_row + ig_row, NEG_BIG)
            inter = b_col + m_prev
            m_t = jnp.maximum(inter, jnp.max(D, axis=1, keepdims=True))
            w_intra = jnp.exp(D - m_t)
            w_inter = jnp.exp(inter - m_t)
            qk = lax.dot_general(q, k, (((1,), (1,)), ((), ())), preferred_element_type=F32)
            s = qk * w_intra
            qc = jnp.dot(q, c_ext.astype(BF16), preferred_element_type=F32)
            num = jnp.dot(s.astype(BF16), v, preferred_element_type=F32) + w_inter * qc[:, :dh]
            den = jnp.sum(s, axis=1, keepdims=True) + w_inter * qc[:, dh:dh + 1]
            hh = num / jnp.maximum(jnp.abs(den), jnp.exp(-m_t))
            hh = hh * lax.rsqrt(jnp.mean(hh * hh, axis=1, keepdims=True) + NORM_EPS)
            o = o_ref[rows, cols]
            y_new.append((hh * gain_ref[:, cols] / (1.0 + jnp.exp(-o))).astype(BF16))

            b_last = b_col[L - 1:L, :]
            g_col = b_last - b_col + ig_col
            m_new = jnp.maximum(b_last + m_prev, jnp.max(g_col, axis=0, keepdims=True))
            decay = jnp.exp(b_last + m_prev - m_new)
            wk = jnp.exp(g_col - m_new)
            kw_t = (wk * k.astype(F32)).T.astype(BF16)
            v_ext = jnp.concatenate([v, ones_col], axis=1)
            c_new.append(decay * c_ext + jnp.dot(kw_t, v_ext, preferred_element_type=F32))
            m_new_rows.append(jnp.broadcast_to(m_new, (1, LANES)))

        y_ref[rows, :] = jnp.concatenate(y_new, axis=1)
        for h in range(HEADS):
            c_ref[h] = c_new[h]
            m_ref[h:h + 1, :] = m_new_rows[h]


def _mlstm(pf, pb, gates, gates_t, bias_row, bias_col, head_gain, *, n_sub=1):
    S = pf.shape[0]
    L = n_sub * MLSTM_CHUNK
    return pl.pallas_call(
        functools.partial(_mlstm_kernel, n_sub=n_sub),
        grid=(S // L,),
        in_specs=[pl.BlockSpec((L, GROUP), lambda c: (c, P_MQ)),
                  pl.BlockSpec((L, GROUP), lambda c: (c, P_MK)),
                  pl.BlockSpec((L, GROUP), lambda c: (c, P_MV)),
                  pl.BlockSpec((L, GROUP), lambda c: (c, P_MO)),
                  pl.BlockSpec((L, LANES), lambda c: (c, 0)),
                  pl.BlockSpec((LANES, L), lambda c: (0, c)),
                  pl.BlockSpec((1, LANES), lambda c: (0, 0)),
                  pl.BlockSpec((LANES, LANES), lambda c: (0, 0)),
                  pl.BlockSpec((1, GROUP), lambda c: (0, 0))],
        out_specs=pl.BlockSpec((L, GROUP), lambda c: (c, 0)),
        out_shape=jax.ShapeDtypeStruct((S, GROUP), BF16),
        scratch_shapes=[pltpu.VMEM((HEADS, HEAD_DIM, 2 * HEAD_DIM), F32), pltpu.VMEM((8, LANES), F32)],
        compiler_params=_params("arbitrary"),
        name="mlstm",
    )(pb, pb, pb, pf, gates, gates_t, bias_row, bias_col, head_gain)


def _softplus(z):
    return jnp.maximum(z, 0.0) + jnp.log(1.0 + jnp.exp(-jnp.abs(z)))


def _sb_kernel(q_ref, k_ref, v_ref, u_ref, y_ref, acc_ref, carry_ref, *, tq, tk, heads):
    i = pl.program_id(1)
    dh = HEAD_DIM
    acc_ref[...] = jnp.zeros_like(acc_ref)
    carry_ref[...] = jnp.zeros_like(carry_ref)
    q_pos = i * tq + lax.broadcasted_iota(jnp.int32, (tq, tk), 0)
    k_off = lax.broadcasted_iota(jnp.int32, (tq, tk), 1)

    def block(j, masked):
        rows = pl.ds(pl.multiple_of(j * tk, tk), tk)
        if masked:
            past = (j * tk + k_off) < q_pos
        for hd in range(heads):
            cols = slice(hd * dh, (hd + 1) * dh)
            z = lax.dot_general(q_ref[:, cols], k_ref[rows, cols], (((1,), (1,)), ((), ())),
                                preferred_element_type=F32)
            sp = _softplus(z)
            if masked:
                sp = jnp.where(past, sp, 0.0)
            hi = sp.astype(BF16)
            lo = (sp - hi.astype(F32)).astype(BF16)
            R = (jnp.dot(hi, u_ref[...], preferred_element_type=F32)
                 + jnp.dot(lo, u_ref[...], preferred_element_type=F32))
            carry = carry_ref[hd]
            A = jnp.exp(z - R - carry)
            if masked:
                A = jnp.where(past, A, 0.0)
            acc_ref[:, cols] += jnp.dot(A.astype(BF16), v_ref[rows, cols], preferred_element_type=F32)
            carry_ref[hd] = carry + R[:, 0:1]

    n_diag = tq // tk
    j_hi = (i + 1) * n_diag
    for d in range(n_diag):
        block(j_hi - 1 - d, True)

    def cond(state):
        jj, live = state
        return jnp.logical_and(jj < j_hi - n_diag, live)

    def body(state):
        jj, _ = state
        block(j_hi - n_diag - 1 - jj, False)
        return jj + 1, jnp.min(carry_ref[...]) < SB_DEAD_CARRY

    lax.while_loop(cond, body, (jnp.int32(0), jnp.min(carry_ref[...]) < SB_DEAD_CARRY))
    y_ref[...] = acc_ref[...].astype(BF16)


def _sb_attention(pb, tri_u, *, tq=512, tk=256, heads=2):
    S = pb.shape[0]
    tq = min(tq, S)
    tk = min(tk, tq)
    w = heads * HEAD_DIM
    groups = HEADS // heads
    qcol, kcol, vcol = P_SQ * groups, P_SK * groups, P_SV * groups
    return pl.pallas_call(
        functools.partial(_sb_kernel, tq=tq, tk=tk, heads=heads),
        grid=(groups, S // tq),
        in_specs=[pl.BlockSpec((tq, w), lambda h, i: (i, qcol + h)),
                  pl.BlockSpec((S, w), lambda h, i: (0, kcol + h)),
                  pl.BlockSpec((S, w), lambda h, i: (0, vcol + h)),
                  pl.BlockSpec((tk, tk), lambda h, i: (0, 0))],
        out_specs=pl.BlockSpec((tq, w), lambda h, i: (i, h)),
        out_shape=jax.ShapeDtypeStruct((S, GROUP), BF16),
        scratch_shapes=[pltpu.VMEM((tq, w), F32), pltpu.VMEM((heads, tq, 1), F32)],
        compiler_params=_params("arbitrary", "arbitrary"),
        name="sb_attention",
    )(pb, pb, pb, tri_u)


def _pack_w_in_kernel(wt_ref, main_ref):
    main_ref[...] = wt_ref[0].T.astype(BF16)


def _pack_w_in(w_in):
    depth, D, N = w_in.shape
    G = GROUP
    gate0 = 7 * G
    after = gate0 + 2 * HEADS
    starts = (0, G, 2 * G, 6 * G, after,
              3 * G, 4 * G, 5 * G,
              after + G, after + 2 * G, after + 3 * G)
    w_t = jnp.swapaxes(w_in, 1, 2)

    def panel_start(p):
        s = jnp.int32(starts[0] // 8)
        for idx in range(1, len(starts)):
            s = jnp.where(p == idx, starts[idx] // 8, s)
        return s * 8

    main = pl.pallas_call(
        _pack_w_in_kernel,
        grid=(depth, len(starts)),
        in_specs=[pl.BlockSpec((pl.Element(1), pl.Element(G), pl.Element(D)),
                               lambda l, p: (l, panel_start(p), 0))],
        out_specs=pl.BlockSpec((None, D, G), lambda l, p: (l, 0, p)),
        out_shape=jax.ShapeDtypeStruct((depth, D, len(starts) * G), BF16),
        compiler_params=_params("arbitrary", "arbitrary"),
        name="pack_w_in",
    )(w_t)
    gate_t = jnp.pad(w_t[:, gate0:after, :], ((0, 0), (0, LANES - 2 * HEADS), (0, 0))).astype(BF16)
    return main, jnp.swapaxes(gate_t, 1, 2), gate_t


def _token_mix(h, x, layer, w_main, w_gate, w_gate_t, w_out, conv_w, pool_w, pool_scale, i_bias, f_bias,
               head_gain, g_post, g_next, tri_u):
    pf, pb, gates, gates_t = _in_proj(h, w_main, w_gate, w_gate_t, layer)
    y_conv, y_pool = _conv_pool(pf, conv_w, pool_w, pool_scale[None, :])
    bias = jnp.pad(jnp.concatenate([i_bias, f_bias]), (0, LANES - 2 * HEADS))
    bias_col = jnp.broadcast_to(bias[:, None], (LANES, LANES))
    y_mlstm = _mlstm(pf, pb, gates, gates_t, bias[None, :], bias_col, head_gain[None, :])
    y_sb = _sb_attention(pb, tri_u)
    return _proj_res([y_conv, y_mlstm, y_pool, y_sb], w_out, layer, x, g_post, g_next, res_scale=1.0)


def kernel(x, w_in, w_out, conv_w, pool_w, pool_scale, mlstm_i_bias, mlstm_f_bias, mlstm_head_gain,
           ffn1_w_gate, ffn1_w_up, ffn1_w_down, ffn2_w_gate, ffn2_w_up, ffn2_w_down, norm_gains):
    B, S, D = x.shape
    depth = w_in.shape[0]
    outs = []
    tk = min(256, S)
    tri_u = (jnp.arange(tk)[:, None] >= jnp.arange(tk)[None, :]).astype(BF16)
    w_main, w_gate, w_gate_t = _pack_w_in(w_in)
    w_out_bf = w_out.astype(BF16)
    wd1_bf = ffn1_w_down.astype(BF16)
    wd2_bf = ffn2_w_down.astype(BF16)
    for b in range(B):
        xb = x[b]
        h = _first_norm(xb, norm_gains[0, 0][None, :])
        for l in range(depth):
            g = norm_gains[l]
            g_after = norm_gains[l + 1, 0] if l + 1 < depth else g[0]
            act = _ffn_up(h, ffn1_w_gate, ffn1_w_up, l)
            xb, h = _proj_res([act], wd1_bf, l, xb, g[1][None, :], g[2][None, :], res_scale=0.5)
            xb, h = _token_mix(h, xb, l, w_main, w_gate, w_gate_t, w_out_bf, conv_w[l], pool_w[l],
                               pool_scale[l], mlstm_i_bias[l], mlstm_f_bias[l], mlstm_head_gain[l],
                               g[3][None, :], g[4][None, :], tri_u)
            act = _ffn_up(h, ffn2_w_gate, ffn2_w_up, l)
            xb, h = _proj_res([act], wd2_bf, l, xb, g[5][None, :], g_after[None, :], res_scale=0.5)
        outs.append(xb)
    return jnp.stack(outs, axis=0)
```

```python
import functools

import jax
import jax.numpy as jnp
from jax import lax
from jax.experimental import pallas as pl
from jax.experimental.pallas import tpu as pltpu

F32 = jnp.float32
BF16 = jnp.bfloat16

NORM_EPS = 1e-6
HEAD_DIM = 128
HEADS = 4
GROUP = HEADS * HEAD_DIM
CONV_WIDTH = 3
POOL_WINDOWS = (2, 4, 8, 16)
POOL_HALO = 16
MLSTM_CHUNK = 128
LANES = 128
NEG_BIG = -1e30
SB_DEAD_CARRY = 105.0

VMEM_LIMIT = 56 * 1024 * 1024

F32_PANELS = 5
BF16_PANELS = 6
P_CB, P_CC, P_CU, P_MO, P_PU = range(F32_PANELS)
P_MQ, P_MK, P_MV, P_SQ, P_SK, P_SV = range(BF16_PANELS)


def _params(*sem):
    return pltpu.CompilerParams(dimension_semantics=sem, vmem_limit_bytes=VMEM_LIMIT)


def _rms_rows(y, g):
    ms = jnp.mean(y * y, axis=-1, keepdims=True)
    return y * lax.rsqrt(ms + NORM_EPS) * g


def _split3(x):
    hi = x.astype(BF16)
    r1 = x - hi.astype(F32)
    mid = r1.astype(BF16)
    lo = (r1 - mid.astype(F32)).astype(BF16)
    return hi, mid, lo


def _norm_kernel(x_ref, g_ref, h_ref):
    h_ref[...] = _rms_rows(x_ref[...], g_ref[...]).astype(BF16)


def _first_norm(x, g):
    S, D = x.shape
    tm = 256
    return pl.pallas_call(
        _norm_kernel,
        grid=(S // tm,),
        in_specs=[pl.BlockSpec((tm, D), lambda i: (i, 0)),
                  pl.BlockSpec((1, D), lambda i: (0, 0))],
        out_specs=pl.BlockSpec((tm, D), lambda i: (i, 0)),
        out_shape=jax.ShapeDtypeStruct((S, D), BF16),
        compiler_params=_params("arbitrary"),
        name="first_norm",
    )(x, g)


def _ffn_up_kernel(h_ref, wg_ref, wu_ref, o_ref, wg_bf, wu_bf, *, row_chunk):
    @pl.when(pl.program_id(1) == 0)
    def _():
        wg_bf[...] = wg_ref[...].astype(BF16)
        wu_bf[...] = wu_ref[...].astype(BF16)

    def body(r, carry):
        rows = pl.ds(pl.multiple_of(r * row_chunk, row_chunk), row_chunk)
        h = h_ref[rows, :]
        g = jnp.dot(h, wg_bf[...], preferred_element_type=F32)
        u = jnp.dot(h, wu_bf[...], preferred_element_type=F32)
        o_ref[rows, :] = (g / (1.0 + jnp.exp(-g)) * u).astype(BF16)
        return carry

    lax.fori_loop(0, h_ref.shape[0] // row_chunk, body, 0)


def _ffn_up(h, wg, wu, layer, *, tm=2048, tn=512, row_chunk=1024):
    S, D = h.shape
    N = wg.shape[2]
    tm = min(tm, S)
    return pl.pallas_call(
        functools.partial(_ffn_up_kernel, row_chunk=min(row_chunk, tm)),
        grid=(N // tn, S // tm),
        in_specs=[pl.BlockSpec((tm, D), lambda j, i: (i, 0)),
                  pl.BlockSpec((None, D, tn), lambda j, i: (layer, 0, j)),
                  pl.BlockSpec((None, D, tn), lambda j, i: (layer, 0, j))],
        out_specs=pl.BlockSpec((tm, tn), lambda j, i: (i, j)),
        out_shape=jax.ShapeDtypeStruct((S, N), BF16),
        scratch_shapes=[pltpu.VMEM((D, tn), BF16), pltpu.VMEM((D, tn), BF16)],
        compiler_params=_params("arbitrary", "arbitrary"),
        name="ffn_up",
    )(h, wg, wu)


def _proj_res_kernel(*refs, n_lhs, nk, res_scale, row_chunk, ep_chunk):
    a_refs = refs[:n_lhs]
    w_ref, x_ref, gp_ref, gn_ref, xo_ref, ho_ref = refs[n_lhs:]
    k = pl.program_id(1)
    tm = xo_ref.shape[0]
    kw = a_refs[0].shape[1]

    def accumulate(first):
        def body(r, carry):
            rows = pl.ds(pl.multiple_of(r * row_chunk, row_chunk), row_chunk)
            part = jnp.dot(a_refs[0][rows, :], w_ref[0:kw, :], preferred_element_type=F32)
            for g in range(1, n_lhs):
                part = part + jnp.dot(a_refs[g][rows, :], w_ref[g * kw:(g + 1) * kw, :],
                                      preferred_element_type=F32)
            if first:
                xo_ref[rows, :] = part
            else:
                xo_ref[rows, :] += part
            return carry
        lax.fori_loop(0, tm // row_chunk, body, 0)

    def epilogue():
        def body(r, carry):
            rows = pl.ds(pl.multiple_of(r * ep_chunk, ep_chunk), ep_chunk)
            xn = x_ref[rows, :] + res_scale * _rms_rows(xo_ref[rows, :], gp_ref[...])
            xo_ref[rows, :] = xn
            ho_ref[rows, :] = _rms_rows(xn, gn_ref[...]).astype(BF16)
            return carry
        lax.fori_loop(0, tm // ep_chunk, body, 0, unroll=2)

    def last_step(first):
        half = tm // 2
        for c in range(2):
            rows = slice(c * half, (c + 1) * half)
            part = jnp.dot(a_refs[0][rows, :], w_ref[0:kw, :], preferred_element_type=F32)
            for g in range(1, n_lhs):
                part = part + jnp.dot(a_refs[g][rows, :], w_ref[g * kw:(g + 1) * kw, :],
                                      preferred_element_type=F32)
            xo_ref[rows, :] = part if first else xo_ref[rows, :] + part
            for e in range(half // ep_chunk):
                er = slice(c * half + e * ep_chunk, c * half + (e + 1) * ep_chunk)
                xn = x_ref[er, :] + res_scale * _rms_rows(xo_ref[er, :], gp_ref[...])
                xo_ref[er, :] = xn
                ho_ref[er, :] = _rms_rows(xn, gn_ref[...]).astype(BF16)

    if nk == 1:
        last_step(True)
    else:
        pl.when(k == 0)(lambda: accumulate(True))
        pl.when(jnp.logical_and(k > 0, k < nk - 1))(lambda: accumulate(False))
        pl.when(k == nk - 1)(lambda: last_step(False))


def _proj_res(lhs, w, layer, x, g_post, g_next, *, res_scale, tm=512, tk=1408, row_chunk=512, ep_chunk=64):
    S, D = x.shape
    tm = min(tm, S)
    n_lhs = len(lhs)
    kw = lhs[0].shape[1]
    if n_lhs > 1:
        tk = n_lhs * kw
    nk = w.shape[1] // tk
    return pl.pallas_call(
        functools.partial(_proj_res_kernel, n_lhs=n_lhs, nk=nk, res_scale=res_scale,
                          row_chunk=min(row_chunk, tm), ep_chunk=ep_chunk),
        grid=(S // tm, nk),
        in_specs=[pl.BlockSpec((tm, tk // n_lhs), lambda i, k: (i, k)) for _ in lhs] + [
            pl.BlockSpec((None, tk, D), lambda i, k: (layer, k, 0)),
            pl.BlockSpec((tm, D), lambda i, k: (i, 0)),
            pl.BlockSpec((1, D), lambda i, k: (0, 0)),
            pl.BlockSpec((1, D), lambda i, k: (0, 0))],
        out_specs=[pl.BlockSpec((tm, D), lambda i, k: (i, 0)),
                   pl.BlockSpec((tm, D), lambda i, k: (i, 0))],
        out_shape=[jax.ShapeDtypeStruct((S, D), F32), jax.ShapeDtypeStruct((S, D), BF16)],
        compiler_params=_params("arbitrary", "arbitrary"),
        name="proj_res",
    )(*lhs, w, x, g_post, g_next)


def _in_proj_kernel(h_ref, w_ref, wg_ref, wgt_ref, of_ref, ob_ref, g_ref, gt_ref, *, scale):
    h = h_ref[...]
    for p in range(F32_PANELS):
        cols = slice(p * GROUP, (p + 1) * GROUP)
        of_ref[:, cols] = jnp.dot(h, w_ref[:, cols], preferred_element_type=F32)
    for p in range(BF16_PANELS):
        cols = slice(p * GROUP, (p + 1) * GROUP)
        wcols = slice((F32_PANELS + p) * GROUP, (F32_PANELS + p + 1) * GROUP)
        r = jnp.dot(h, w_ref[:, wcols], preferred_element_type=F32)
        if p in (P_MK, P_SQ):
            r = r * scale
        ob_ref[:, cols] = r.astype(BF16)
    g_ref[...] = jnp.dot(h, wg_ref[...], preferred_element_type=F32)
    gt_ref[...] = lax.dot_general(wgt_ref[...], h, (((1,), (1,)), ((), ())), preferred_element_type=F32)


def _in_proj(h, w_main, w_gate, w_gate_t, layer, *, tm=512):
    S, D = h.shape
    tm = min(tm, S)
    nf, nb = F32_PANELS * GROUP, BF16_PANELS * GROUP
    return pl.pallas_call(
        functools.partial(_in_proj_kernel, scale=HEAD_DIM ** -0.5),
        grid=(S // tm,),
        in_specs=[pl.BlockSpec((tm, D), lambda i: (i, 0)),
                  pl.BlockSpec((None, D, nf + nb), lambda i: (layer, 0, 0), pipeline_mode=pl.Buffered(1)),
                  pl.BlockSpec((None, D, LANES), lambda i: (layer, 0, 0)),
                  pl.BlockSpec((None, LANES, D), lambda i: (layer, 0, 0))],
        out_specs=[pl.BlockSpec((tm, nf), lambda i: (i, 0)),
                   pl.BlockSpec((tm, nb), lambda i: (i, 0)),
                   pl.BlockSpec((tm, LANES), lambda i: (i, 0)),
                   pl.BlockSpec((LANES, tm), lambda i: (0, i))],
        out_shape=[jax.ShapeDtypeStruct((S, nf), F32),
                   jax.ShapeDtypeStruct((S, nb), BF16),
                   jax.ShapeDtypeStruct((S, LANES), F32),
                   jax.ShapeDtypeStruct((LANES, S), F32)],
        compiler_params=_params("arbitrary"),
        name="in_proj",
    )(h, w_main, w_gate, w_gate_t)


def _conv_pool_kernel(cb_ref, cc_ref, cu_ref, pu_ref, cch_ref, cuh_ref, puh_ref,
                      cw_ref, pw_ref, ps_ref, yc_ref, yp_ref, zbuf, pbuf):
    i = pl.program_id(0)
    tm = cb_ref.shape[0]
    keep = (i > 0).astype(F32)
    H = POOL_HALO

    zbuf[0:H, :] = cch_ref[...] * cuh_ref[...] * keep
    zbuf[H:H + tm, :] = cc_ref[...] * cu_ref[...]
    y = cw_ref[0:1, :] * zbuf[H - 2:H - 2 + tm, :]
    y = y + cw_ref[1:2, :] * zbuf[H - 1:H - 1 + tm, :]
    y = y + cw_ref[2:3, :] * zbuf[H:H + tm, :]
    yc_ref[...] = (cb_ref[...] * y).astype(BF16)

    pbuf[0:H, :] = puh_ref[...] * keep
    pbuf[H:H + tm, :] = pu_ref[...]
    t = i * tm + lax.broadcasted_iota(jnp.int32, (tm, 1), 0)
    for g, win in enumerate(POOL_WINDOWS):
        cols = slice(g * LANES, (g + 1) * LANES)
        u = pbuf[H:H + tm, cols]
        wsum = u
        for d in range(1, win):
            wsum = wsum + pbuf[H - d:H - d + tm, cols]
        count = jnp.minimum(t + 1, win).astype(F32)
        pooled = wsum / count - u
        yg = jnp.dot(pooled.astype(BF16), pw_ref[g].astype(BF16), preferred_element_type=F32)
        yp_ref[:, cols] = (yg * ps_ref[:, cols]).astype(BF16)


def _conv_pool(pf, conv_w, pool_w, pool_scale, *, tm=512):
    S = pf.shape[0]
    tm = min(tm, S)
    H = POOL_HALO
    hb = tm // H

    def cur(p):
        return pl.BlockSpec((tm, GROUP), lambda i, p=p: (i, p))

    def halo(p):
        return pl.BlockSpec((H, GROUP), lambda i, p=p: (jnp.maximum(i * hb - 1, 0), p))

    return pl.pallas_call(
        _conv_pool_kernel,
        grid=(S // tm,),
        in_specs=[cur(P_CB), cur(P_CC), cur(P_CU), cur(P_PU), halo(P_CC), halo(P_CU), halo(P_PU),
                  pl.BlockSpec((CONV_WIDTH, GROUP), lambda i: (0, 0)),
                  pl.BlockSpec(pool_w.shape, lambda i: (0, 0, 0)),
                  pl.BlockSpec((1, GROUP), lambda i: (0, 0))],
        out_specs=[pl.BlockSpec((tm, GROUP), lambda i: (i, 0)),
                   pl.BlockSpec((tm, GROUP), lambda i: (i, 0))],
        out_shape=[jax.ShapeDtypeStruct((S, GROUP), BF16), jax.ShapeDtypeStruct((S, GROUP), BF16)],
        scratch_shapes=[pltpu.VMEM((H + tm, GROUP), F32), pltpu.VMEM((H + tm, GROUP), F32)],
        compiler_params=_params("arbitrary"),
        name="conv_pool",
    )(pf, pf, pf, pf, pf, pf, pf, conv_w, pool_w, pool_scale)


def _log_sigmoid(x):
    return jnp.minimum(x, 0.0) - jnp.log(1.0 + jnp.exp(-jnp.abs(x)))


def _mlstm_kernel(q_ref, k_ref, v_ref, o_ref, g_ref, gt_ref, brow_ref, bcol_ref, gain_ref,
                  y_ref, c_ref, m_ref, *, n_sub):
    L = MLSTM_CHUNK
    dh = HEAD_DIM

    @pl.when(pl.program_id(0) == 0)
    def _():
        c_ref[...] = jnp.zeros_like(c_ref)
        m_ref[...] = jnp.zeros_like(m_ref)

    row = lax.broadcasted_iota(jnp.int32, (L, L), 0)
    col = lax.broadcasted_iota(jnp.int32, (L, L), 1)
    causal = col <= row
    tri = causal.astype(BF16)
    tri_t = (row <= col).astype(BF16)
    ones_col = (lax.broadcasted_iota(jnp.int32, (L, dh), 1) == 0).astype(BF16)

    for sub in range(n_sub):
        rows = slice(sub * L, (sub + 1) * L)
        G = g_ref[rows, :] + brow_ref[...]
        GT = gt_ref[:, rows] + bcol_ref[...]
        bc = sum(jnp.dot(tri, part, preferred_element_type=F32) for part in _split3(_log_sigmoid(G)))
        br = sum(jnp.dot(part, tri_t, preferred_element_type=F32) for part in _split3(_log_sigmoid(GT)))

        y_new, c_new, m_new_rows = [], [], []
        for h in range(HEADS):
            cols = slice(h * dh, (h + 1) * dh)
            q = q_ref[rows, cols]
            k = k_ref[rows, cols]
            v = v_ref[rows, cols]
            b_col = bc[:, HEADS + h:HEADS + h + 1]
            b_row = br[HEADS + h:HEADS + h + 1, :]
            ig_col = G[:, h:h + 1]
            ig_row = GT[h:h + 1, :]
            m_prev = m_ref[h:h + 1, 0:1]
            c_ext = c_ref[h]

            D = jnp.where(causal, b_col - b_row + ig_row, NEG_BIG)
            inter = b_col + m_prev
            m_t = jnp.maximum(inter, jnp.max(D, axis=1, keepdims=True))
            w_intra = jnp.exp(D - m_t)
            w_inter = jnp.exp(inter - m_t)
            qk = lax.dot_general(q, k, (((1,), (1,)), ((), ())), preferred_element_type=F32)
            s = qk * w_intra
            qc = jnp.dot(q, c_ext.astype(BF16), preferred_element_type=F32)
            num = jnp.dot(s.astype(BF16), v, preferred_element_type=F32) + w_inter * qc[:, :dh]
            den = jnp.sum(s, axis=1, keepdims=True) + w_inter * qc[:, dh:dh + 1]
            hh = num / jnp.maximum(jnp.abs(den), jnp.exp(-m_t))
            hh = hh * lax.rsqrt(jnp.mean(hh * hh, axis=1, keepdims=True) + NORM_EPS)
            o = o_ref[rows, cols]
            y_new.append((hh * gain_ref[:, cols] / (1.0 + jnp.exp(-o))).astype(BF16))

            b_last = b_col[L - 1:L, :]
            g_col = b_last - b_col + ig_col
            m_new = jnp.maximum(b_last + m_prev, jnp.max(g_col, axis=0, keepdims=True))
            decay = jnp.exp(b_last + m_prev - m_new)
            wk = jnp.exp(g_col - m_new)
            kw_t = (wk * k.astype(F32)).T.astype(BF16)
            v_ext = jnp.concatenate([v, ones_col], axis=1)
            c_new.append(decay * c_ext + jnp.dot(kw_t, v_ext, preferred_element_type=F32))
            m_new_rows.append(jnp.broadcast_to(m_new, (1, LANES)))

        y_ref[rows, :] = jnp.concatenate(y_new, axis=1)
        for h in range(HEADS):
            c_ref[h] = c_new[h]
            m_ref[h:h + 1, :] = m_new_rows[h]


def _mlstm(pf, pb, gates, gates_t, bias_row, bias_col, head_gain, *, n_sub=1):
    S = pf.shape[0]
    L = n_sub * MLSTM_CHUNK
    return pl.pallas_call(
        functools.partial(_mlstm_kernel, n_sub=n_sub),
        grid=(S // L,),
        in_specs=[pl.BlockSpec((L, GROUP), lambda c: (c, P_MQ)),
                  pl.BlockSpec((L, GROUP), lambda c: (c, P_MK)),
                  pl.BlockSpec((L, GROUP), lambda c: (c, P_MV)),
                  pl.BlockSpec((L, GROUP), lambda c: (c, P_MO)),
                  pl.BlockSpec((L, LANES), lambda c: (c, 0)),
                  pl.BlockSpec((LANES, L), lambda c: (0, c)),
                  pl.BlockSpec((1, LANES), lambda c: (0, 0)),
                  pl.BlockSpec((LANES, LANES), lambda c: (0, 0)),
                  pl.BlockSpec((1, GROUP), lambda c: (0, 0))],
        out_specs=pl.BlockSpec((L, GROUP), lambda c: (c, 0)),
        out_shape=jax.ShapeDtypeStruct((S, GROUP), BF16),
        scratch_shapes=[pltpu.VMEM((HEADS, HEAD_DIM, 2 * HEAD_DIM), F32), pltpu.VMEM((8, LANES), F32)],
        compiler_params=_params("arbitrary"),
        name="mlstm",
    )(pb, pb, pb, pf, gates, gates_t, bias_row, bias_col, head_gain)


def _softplus(z):
    return jnp.maximum(z, 0.0) + jnp.log(1.0 + jnp.exp(-jnp.abs(z)))


def _sb_kernel(q_ref, k_ref, v_ref, u_ref, y_ref, acc_ref, carry_ref, *, tq, tk, heads):
    i = pl.program_id(1)
    dh = HEAD_DIM
    acc_ref[...] = jnp.zeros_like(acc_ref)
    carry_ref[...] = jnp.zeros_like(carry_ref)
    q_pos = i * tq + lax.broadcasted_iota(jnp.int32, (tq, tk), 0)
    k_off = lax.broadcasted_iota(jnp.int32, (tq, tk), 1)

    def block(j, masked):
        rows = pl.ds(pl.multiple_of(j * tk, tk), tk)
        if masked:
            past = (j * tk + k_off) < q_pos
        for hd in range(heads):
            cols = slice(hd * dh, (hd + 1) * dh)
            z = lax.dot_general(q_ref[:, cols], k_ref[rows, cols], (((1,), (1,)), ((), ())),
                                preferred_element_type=F32)
            sp = _softplus(z)
            if masked:
                sp = jnp.where(past, sp, 0.0)
            hi = sp.astype(BF16)
            lo = (sp - hi.astype(F32)).astype(BF16)
            R = (jnp.dot(hi, u_ref[...], preferred_element_type=F32)
                 + jnp.dot(lo, u_ref[...], preferred_element_type=F32))
            carry = carry_ref[hd]
            A = jnp.exp(z - R - carry)
            if masked:
                A = jnp.where(past, A, 0.0)
            acc_ref[:, cols] += jnp.dot(A.astype(BF16), v_ref[rows, cols], preferred_element_type=F32)
            carry_ref[hd] = carry + R[:, 0:1]

    n_diag = tq // tk
    j_hi = (i + 1) * n_diag
    for d in range(n_diag):
        block(j_hi - 1 - d, True)

    def cond(state):
        jj, live = state
        return jnp.logical_and(jj < j_hi - n_diag, live)

    def body(state):
        jj, _ = state
        block(j_hi - n_diag - 1 - jj, False)
        return jj + 1, jnp.min(carry_ref[...]) < SB_DEAD_CARRY

    lax.while_loop(cond, body, (jnp.int32(0), jnp.min(carry_ref[...]) < SB_DEAD_CARRY))
    y_ref[...] = acc_ref[...].astype(BF16)


def _sb_attention(pb, tri_u, *, tq=512, tk=256, heads=2):
    S = pb.shape[0]
    tq = min(tq, S)
    tk = min(tk, tq)
    w = heads * HEAD_DIM
    groups = HEADS // heads
    qcol, kcol, vcol = P_SQ * groups, P_SK * groups, P_SV * groups
    return pl.pallas_call(
        functools.partial(_sb_kernel, tq=tq, tk=tk, heads=heads),
        grid=(groups, S // tq),
        in_specs=[pl.BlockSpec((tq, w), lambda h, i: (i, qcol + h)),
                  pl.BlockSpec((S, w), lambda h, i: (0, kcol + h)),
                  pl.BlockSpec((S, w), lambda h, i: (0, vcol + h)),
                  pl.BlockSpec((tk, tk), lambda h, i: (0, 0))],
        out_specs=pl.BlockSpec((tq, w), lambda h, i: (i, h)),
        out_shape=jax.ShapeDtypeStruct((S, GROUP), BF16),
        scratch_shapes=[pltpu.VMEM((tq, w), F32), pltpu.VMEM((heads, tq, 1), F32)],
        compiler_params=_params("arbitrary", "arbitrary"),
        name="sb_attention",
    )(pb, pb, pb, tri_u)


def _pack_w_in_kernel(wt_ref, main_ref):
    main_ref[...] = wt_ref[0].T.astype(BF16)


def _pack_w_in(w_in):
    depth, D, N = w_in.shape
    G = GROUP
    gate0 = 7 * G
    after = gate0 + 2 * HEADS
    starts = (0, G, 2 * G, 6 * G, after,
              3 * G, 4 * G, 5 * G,
              after + G, after + 2 * G, after + 3 * G)
    w_t = jnp.swapaxes(w_in, 1, 2)

    def panel_start(p):
        s = jnp.int32(starts[0] // 8)
        for idx in range(1, len(starts)):
            s = jnp.where(p == idx, starts[idx] // 8, s)
        return s * 8

    main = pl.pallas_call(
        _pack_w_in_kernel,
        grid=(depth, len(starts)),
        in_specs=[pl.BlockSpec((pl.Element(1), pl.Element(G), pl.Element(D)),
                               lambda l, p: (l, panel_start(p), 0))],
        out_specs=pl.BlockSpec((None, D, G), lambda l, p: (l, 0, p)),
        out_shape=jax.ShapeDtypeStruct((depth, D, len(starts) * G), BF16),
        compiler_params=_params("arbitrary", "arbitrary"),
        name="pack_w_in",
    )(w_t)
    gate_t = jnp.pad(w_t[:, gate0:after, :], ((0, 0), (0, LANES - 2 * HEADS), (0, 0))).astype(BF16)
    return main, jnp.swapaxes(gate_t, 1, 2), gate_t


def _token_mix(h, x, layer, w_main, w_gate, w_gate_t, w_out, conv_w, pool_w, pool_scale, i_bias, f_bias,
               head_gain, g_post, g_next, tri_u):
    pf, pb, gates, gates_t = _in_proj(h, w_main, w_gate, w_gate_t, layer)
    y_conv, y_pool = _conv_pool(pf, conv_w, pool_w, pool_scale[None, :])
    bias = jnp.pad(jnp.concatenate([i_bias, f_bias]), (0, LANES - 2 * HEADS))
    bias_col = jnp.broadcast_to(bias[:, None], (LANES, LANES))
    y_mlstm = _mlstm(pf, pb, gates, gates_t, bias[None, :], bias_col, head_gain[None, :])
    y_sb = _sb_attention(pb, tri_u)
    return _proj_res([y_conv, y_mlstm, y_pool, y_sb], w_out, layer, x, g_post, g_next, res_scale=1.0)


def kernel(x, w_in, w_out, conv_w, pool_w, pool_scale, mlstm_i_bias, mlstm_f_bias, mlstm_head_gain,
           ffn1_w_gate, ffn1_w_up, ffn1_w_down, ffn2_w_gate, ffn2_w_up, ffn2_w_down, norm_gains):
    B, S, D = x.shape
    depth = w_in.shape[0]
    outs = []
    tk = min(256, S)
    tri_u = (jnp.arange(tk)[:, None] >= jnp.arange(tk)[None, :]).astype(BF16)
    w_main, w_gate, w_gate_t = _pack_w_in(w_in)
    w_out_bf = w_out.astype(BF16)
    wd1_bf = ffn1_w_down.astype(BF16)
    wd2_bf = ffn2_w_down.astype(BF16)
    for b in range(B):
        xb = x[b]
        h = _first_norm(xb, norm_gains[0, 0][None, :])
        for l in range(depth):
            g = norm_gains[l]
            g_after = norm_gains[l + 1, 0] if l + 1 < depth else g[0]
            act = _ffn_up(h, ffn1_w_gate, ffn1_w_up, l)
            xb, h = _proj_res([act], wd1_bf, l, xb, g[1][None, :], g[2][None, :], res_scale=0.5)
            xb, h = _token_mix(h, xb, l, w_main, w_gate, w_gate_t, w_out_bf, conv_w[l], pool_w[l],
                               pool_scale[l], mlstm_i_bias[l], mlstm_f_bias[l], mlstm_head_gain[l],
                               g[3][None, :], g[4][None, :], tri_u)
            act = _ffn_up(h, ffn2_w_gate, ffn2_w_up, l)
            xb, h = _proj_res([act], wd2_bf, l, xb, g[5][None, :], g_after[None, :], res_scale=0.5)
        outs.append(xb)
    return jnp.stack(outs, axis=0)
```

```python
import functools

import jax
import jax.numpy as jnp
from jax import lax
from jax.experimental import pallas as pl
from jax.experimental.pallas import tpu as pltpu

F32 = jnp.float32
BF16 = jnp.bfloat16

NORM_EPS = 1e-6
HEAD_DIM = 128
HEADS = 4
GROUP = HEADS * HEAD_DIM
CONV_WIDTH = 3
POOL_WINDOWS = (2, 4, 8, 16)
POOL_HALO = 16
MLSTM_CHUNK = 128
LANES = 128
NEG_BIG = -1e30
SB_DEAD_CARRY = 105.0

VMEM_LIMIT = 56 * 1024 * 1024

F32_PANELS = 5
BF16_PANELS = 6
P_CB, P_CC, P_CU, P_MO, P_PU = range(F32_PANELS)
P_MQ, P_MK, P_MV, P_SQ, P_SK, P_SV = range(BF16_PANELS)


def _params(*sem):
    return pltpu.CompilerParams(dimension_semantics=sem, vmem_limit_bytes=VMEM_LIMIT)


def _rms_rows(y, g):
    ms = jnp.mean(y * y, axis=-1, keepdims=True)
    return y * lax.rsqrt(ms + NORM_EPS) * g


def _split3(x):
    hi = x.astype(BF16)
    r1 = x - hi.astype(F32)
    mid = r1.astype(BF16)
    lo = (r1 - mid.astype(F32)).astype(BF16)
    return hi, mid, lo


def _norm_kernel(x_ref, g_ref, h_ref):
    h_ref[...] = _rms_rows(x_ref[...], g_ref[...]).astype(BF16)


def _first_norm(x, g):
    S, D = x.shape
    tm = 256
    return pl.pallas_call(
        _norm_kernel,
        grid=(S // tm,),
        in_specs=[pl.BlockSpec((tm, D), lambda i: (i, 0)),
                  pl.BlockSpec((1, D), lambda i: (0, 0))],
        out_specs=pl.BlockSpec((tm, D), lambda i: (i, 0)),
        out_shape=jax.ShapeDtypeStruct((S, D), BF16),
        compiler_params=_params("arbitrary"),
        name="first_norm",
    )(x, g)


def _ffn_up_kernel(h_ref, wg_ref, wu_ref, o_ref, wg_bf, wu_bf, *, row_chunk):
    @pl.when(pl.program_id(1) == 0)
    def _():
        wg_bf[...] = wg_ref[...].astype(BF16)
        wu_bf[...] = wu_ref[...].astype(BF16)

    def body(r, carry):
        rows = pl.ds(pl.multiple_of(r * row_chunk, row_chunk), row_chunk)
        h = h_ref[rows, :]
        g = jnp.dot(h, wg_bf[...], preferred_element_type=F32)
        u = jnp.dot(h, wu_bf[...], preferred_element_type=F32)
        o_ref[rows, :] = (g / (1.0 + jnp.exp(-g)) * u).astype(BF16)
        return carry

    lax.fori_loop(0, h_ref.shape[0] // row_chunk, body, 0)


def _ffn_up(h, wg, wu, layer, *, tm=2048, tn=512, row_chunk=1024):
    S, D = h.shape
    N = wg.shape[2]
    tm = min(tm, S)
    return pl.pallas_call(
        functools.partial(_ffn_up_kernel, row_chunk=min(row_chunk, tm)),
        grid=(N // tn, S // tm),
        in_specs=[pl.BlockSpec((tm, D), lambda j, i: (i, 0)),
                  pl.BlockSpec((None, D, tn), lambda j, i: (layer, 0, j)),
                  pl.BlockSpec((None, D, tn), lambda j, i: (layer, 0, j))],
        out_specs=pl.BlockSpec((tm, tn), lambda j, i: (i, j)),
        out_shape=jax.ShapeDtypeStruct((S, N), BF16),
        scratch_shapes=[pltpu.VMEM((D, tn), BF16), pltpu.VMEM((D, tn), BF16)],
        compiler_params=_params("arbitrary", "arbitrary"),
        name="ffn_up",
    )(h, wg, wu)


def _proj_res_kernel(*refs, n_lhs, nk, res_scale, row_chunk, ep_chunk):
    a_refs = refs[:n_lhs]
    w_ref, x_ref, gp_ref, gn_ref, xo_ref, ho_ref = refs[n_lhs:]
    k = pl.program_id(1)
    tm = xo_ref.shape[0]
    kw = a_refs[0].shape[1]

    def accumulate(first):
        def body(r, carry):
            rows = pl.ds(pl.multiple_of(r * row_chunk, row_chunk), row_chunk)
            part = jnp.dot(a_refs[0][rows, :], w_ref[0:kw, :], preferred_element_type=F32)
            for g in range(1, n_lhs):
                part = part + jnp.dot(a_refs[g][rows, :], w_ref[g * kw:(g + 1) * kw, :],
                                      preferred_element_type=F32)
            if first:
                xo_ref[rows, :] = part
            else:
                xo_ref[rows, :] += part
            return carry
        lax.fori_loop(0, tm // row_chunk, body, 0)

    def epilogue():
        def body(r, carry):
            rows = pl.ds(pl.multiple_of(r * ep_chunk, ep_chunk), ep_chunk)
            xn = x_ref[rows, :] + res_scale * _rms_rows(xo_ref[rows, :], gp_ref[...])
            xo_ref[rows, :] = xn
            ho_ref[rows, :] = _rms_rows(xn, gn_ref[...]).astype(BF16)
            return carry
        lax.fori_loop(0, tm // ep_chunk, body, 0, unroll=2)

    def last_step(first):
        half = tm // 2
        for c in range(2):
            rows = slice(c * half, (c + 1) * half)
            part = jnp.dot(a_refs[0][rows, :], w_ref[0:kw, :], preferred_element_type=F32)
            for g in range(1, n_lhs):
                part = part + jnp.dot(a_refs[g][rows, :], w_ref[g * kw:(g + 1) * kw, :],
                                      preferred_element_type=F32)
            xo_ref[rows, :] = part if first else xo_ref[rows, :] + part
            for e in range(half // ep_chunk):
                er = slice(c * half + e * ep_chunk, c * half + (e + 1) * ep_chunk)
                xn = x_ref[er, :] + res_scale * _rms_rows(xo_ref[er, :], gp_ref[...])
                xo_ref[er, :] = xn
                ho_ref[er, :] = _rms_rows(xn, gn_ref[...]).astype(BF16)

    if nk == 1:
        last_step(True)
    else:
        pl.when(k == 0)(lambda: accumulate(True))
        pl.when(jnp.logical_and(k > 0, k < nk - 1))(lambda: accumulate(False))
        pl.when(k == nk - 1)(lambda: last_step(False))


def _proj_res(lhs, w, layer, x, g_post, g_next, *, res_scale, tm=512, tk=1408, row_chunk=512, ep_chunk=64):
    S, D = x.shape
    tm = min(tm, S)
    n_lhs = len(lhs)
    kw = lhs[0].shape[1]
    if n_lhs > 1:
        tk = n_lhs * kw
    nk = w.shape[1] // tk
    return pl.pallas_call(
        functools.partial(_proj_res_kernel, n_lhs=n_lhs, nk=nk, res_scale=res_scale,
                          row_chunk=min(row_chunk, tm), ep_chunk=ep_chunk),
        grid=(S // tm, nk),
        in_specs=[pl.BlockSpec((tm, tk // n_lhs), lambda i, k: (i, k)) for _ in lhs] + [
            pl.BlockSpec((None, tk, D), lambda i, k: (layer, k, 0)),
            pl.BlockSpec((tm, D), lambda i, k: (i, 0)),
            pl.BlockSpec((1, D), lambda i, k: (0, 0)),
            pl.BlockSpec((1, D), lambda i, k: (0, 0))],
        out_specs=[pl.BlockSpec((tm, D), lambda i, k: (i, 0)),
                   pl.BlockSpec((tm, D), lambda i, k: (i, 0))],
        out_shape=[jax.ShapeDtypeStruct((S, D), F32), jax.ShapeDtypeStruct((S, D), BF16)],
        compiler_params=_params("arbitrary", "arbitrary"),
        name="proj_res",
    )(*lhs, w, x, g_post, g_next)


def _in_proj_kernel(h_ref, w_ref, wg_ref, wgt_ref, of_ref, ob_ref, g_ref, gt_ref, *, scale):
    h = h_ref[...]
    for p in range(F32_PANELS):
        cols = slice(p * GROUP, (p + 1) * GROUP)
        of_ref[:, cols] = jnp.dot(h, w_ref[:, cols], preferred_element_type=F32)
    for p in range(BF16_PANELS):
        cols = slice(p * GROUP, (p + 1) * GROUP)
        wcols = slice((F32_PANELS + p) * GROUP, (F32_PANELS + p + 1) * GROUP)
        r = jnp.dot(h, w_ref[:, wcols], preferred_element_type=F32)
        if p in (P_MK, P_SQ):
            r = r * scale
        ob_ref[:, cols] = r.astype(BF16)
    g_ref[...] = jnp.dot(h, wg_ref[...], preferred_element_type=F32)
    gt_ref[...] = lax.dot_general(wgt_ref[...], h, (((1,), (1,)), ((), ())), preferred_element_type=F32)


def _in_proj(h, w_main, w_gate, w_gate_t, layer, *, tm=512):
    S, D = h.shape
    tm = min(tm, S)
    nf, nb = F32_PANELS * GROUP, BF16_PANELS * GROUP
    return pl.pallas_call(
        functools.partial(_in_proj_kernel, scale=HEAD_DIM ** -0.5),
        grid=(S // tm,),
        in_specs=[pl.BlockSpec((tm, D), lambda i: (i, 0)),
                  pl.BlockSpec((None, D, nf + nb), lambda i: (layer, 0, 0), pipeline_mode=pl.Buffered(1)),
                  pl.BlockSpec((None, D, LANES), lambda i: (layer, 0, 0)),
                  pl.BlockSpec((None, LANES, D), lambda i: (layer, 0, 0))],
        out_specs=[pl.BlockSpec((tm, nf), lambda i: (i, 0)),
                   pl.BlockSpec((tm, nb), lambda i: (i, 0)),
                   pl.BlockSpec((tm, LANES), lambda i: (i, 0)),
                   pl.BlockSpec((LANES, tm), lambda i: (0, i))],
        out_shape=[jax.ShapeDtypeStruct((S, nf), F32),
                   jax.ShapeDtypeStruct((S, nb), BF16),
                   jax.ShapeDtypeStruct((S, LANES), F32),
                   jax.ShapeDtypeStruct((LANES, S), F32)],
        compiler_params=_params("arbitrary"),
        name="in_proj",
    )(h, w_main, w_gate, w_gate_t)


def _conv_pool_kernel(cb_ref, cc_ref, cu_ref, pu_ref, cch_ref, cuh_ref, puh_ref,
                      cw_ref, pw_ref, ps_ref, yc_ref, yp_ref, zbuf, pbuf):
    i = pl.program_id(0)
    tm = cb_ref.shape[0]
    keep = (i > 0).astype(F32)
    H = POOL_HALO

    zbuf[0:H, :] = cch_ref[...] * cuh_ref[...] * keep
    zbuf[H:H + tm, :] = cc_ref[...] * cu_ref[...]
    y = cw_ref[0:1, :] * zbuf[H - 2:H - 2 + tm, :]
    y = y + cw_ref[1:2, :] * zbuf[H - 1:H - 1 + tm, :]
    y = y + cw_ref[2:3, :] * zbuf[H:H + tm, :]
    yc_ref[...] = (cb_ref[...] * y).astype(BF16)

    pbuf[0:H, :] = puh_ref[...] * keep
    pbuf[H:H + tm, :] = pu_ref[...]
    t = i * tm + lax.broadcasted_iota(jnp.int32, (tm, 1), 0)
    for g, win in enumerate(POOL_WINDOWS):
        cols = slice(g * LANES, (g + 1) * LANES)
        u = pbuf[H:H + tm, cols]
        wsum = u
        for d in range(1, win):
            wsum = wsum + pbuf[H - d:H - d + tm, cols]
        count = jnp.minimum(t + 1, win).astype(F32)
        pooled = wsum / count - u
        yg = jnp.dot(pooled.astype(BF16), pw_ref[g].astype(BF16), preferred_element_type=F32)
        yp_ref[:, cols] = (yg * ps_ref[:, cols]).astype(BF16)


def _conv_pool(pf, conv_w, pool_w, pool_scale, *, tm=512):
    S = pf.shape[0]
    tm = min(tm, S)
    H = POOL_HALO
    hb = tm // H

    def cur(p):
        return pl.BlockSpec((tm, GROUP), lambda i, p=p: (i, p))

    def halo(p):
        return pl.BlockSpec((H, GROUP), lambda i, p=p: (jnp.maximum(i * hb - 1, 0), p))

    return pl.pallas_call(
        _conv_pool_kernel,
        grid=(S // tm,),
        in_specs=[cur(P_CB), cur(P_CC), cur(P_CU), cur(P_PU), halo(P_CC), halo(P_CU), halo(P_PU),
                  pl.BlockSpec((CONV_WIDTH, GROUP), lambda i: (0, 0)),
                  pl.BlockSpec(pool_w.shape, lambda i: (0, 0, 0)),
                  pl.BlockSpec((1, GROUP), lambda i: (0, 0))],
        out_specs=[pl.BlockSpec((tm, GROUP), lambda i: (i, 0)),
                   pl.BlockSpec((tm, GROUP), lambda i: (i, 0))],
        out_shape=[jax.ShapeDtypeStruct((S, GROUP), BF16), jax.ShapeDtypeStruct((S, GROUP), BF16)],
        scratch_shapes=[pltpu.VMEM((H + tm, GROUP), F32), pltpu.VMEM((H + tm, GROUP), F32)],
        compiler_params=_params("arbitrary"),
        name="conv_pool",
    )(pf, pf, pf, pf, pf, pf, pf, conv_w, pool_w, pool_scale)


def _log_sigmoid(x):
    return jnp.minimum(x, 0.0) - jnp.log(1.0 + jnp.exp(-jnp.abs(x)))


def _mlstm_kernel(q_ref, k_ref, v_ref, o_ref, g_ref, gt_ref, brow_ref, bcol_ref, gain_ref,
                  y_ref, c_ref, m_ref, *, n_sub):
    L = MLSTM_CHUNK
    dh = HEAD_DIM

    @pl.when(pl.program_id(0) == 0)
    def _():
        c_ref[...] = jnp.zeros_like(c_ref)
        m_ref[...] = jnp.zeros_like(m_ref)

    row = lax.broadcasted_iota(jnp.int32, (L, L), 0)
    col = lax.broadcasted_iota(jnp.int32, (L, L), 1)
    causal = col <= row
    tri = causal.astype(BF16)
    tri_t = (row <= col).astype(BF16)
    ones_col = (lax.broadcasted_iota(jnp.int32, (L, dh), 1) == 0).astype(BF16)

    for sub in range(n_sub):
        rows = slice(sub * L, (sub + 1) * L)
        G = g_ref[rows, :] + brow_ref[...]
        GT = gt_ref[:, rows] + bcol_ref[...]
        bc = sum(jnp.dot(tri, part, preferred_element_type=F32) for part in _split3(_log_sigmoid(G)))
        br = sum(jnp.dot(part, tri_t, preferred_element_type=F32) for part in _split3(_log_sigmoid(GT)))

        y_new, c_new, m_new_rows = [], [], []
        for h in range(HEADS):
            cols = slice(h * dh, (h + 1) * dh)
            q = q_ref[rows, cols]
            k = k_ref[rows, cols]
            v = v_ref[rows, cols]
            b_col = bc[:, HEADS + h:HEADS + h + 1]
            b_row = br[HEADS + h:HEADS + h + 1, :]
            ig_col = G[:, h:h + 1]
            ig_row = GT[h:h + 1, :]
            m_prev = m_ref[h:h + 1, 0:1]
            c_ext = c_ref[h]

            D = jnp.where(causal, b_col - b_row + ig_row, NEG_BIG)
            inter = b_col + m_prev
            m_t = jnp.maximum(inter, jnp.max(D, axis=1, keepdims=True))
            w_intra = jnp.exp(D - m_t)
            w_inter = jnp.exp(inter - m_t)
            qk = lax.dot_general(q, k, (((1,), (1,)), ((), ())), preferred_element_type=F32)
            s = qk * w_intra
            qc = jnp.dot(q, c_ext.astype(BF16), preferred_element_type=F32)
            num = jnp.dot(s.astype(BF16), v, preferred_element_type=F32) + w_inter * qc[:, :dh]
            den = jnp.sum(s, axis=1, keepdims=True) + w_inter * qc[:, dh:dh + 1]
            hh = num / jnp.maximum(jnp.abs(den), jnp.exp(-m_t))
            hh = hh * lax.rsqrt(jnp.mean(hh * hh, axis=1, keepdims=True) + NORM_EPS)
            o = o_ref[rows, cols]
            y_new.append((hh * gain_ref[:, cols] / (1.0 + jnp.exp(-o))).astype(BF16))

            b_last = b_col[L - 1:L, :]
            g_col = b_last - b_col + ig_col
            m_new = jnp.maximum(b_last + m_prev, jnp.max(g_col, axis=0, keepdims=True))
            decay = jnp.exp(b_last + m_prev - m_new)
            wk = jnp.exp(g_col - m_new)
            kw_t = (wk * k.astype(F32)).T.astype(BF16)
            v_ext = jnp.concatenate([v, ones_col], axis=1)
            c_new.append(decay * c_ext + jnp.dot(kw_t, v_ext, preferred_element_type=F32))
            m_new_rows.append(jnp.broadcast_to(m_new, (1, LANES)))

        y_ref[rows, :] = jnp.concatenate(y_new, axis=1)
        for h in range(HEADS):
            c_ref[h] = c_new[h]
            m_ref[h:h + 1, :] = m_new_rows[h]


def _mlstm(pf, pb, gates, gates_t, bias_row, bias_col, head_gain, *, n_sub=1):
    S = pf.shape[0]
    L = n_sub * MLSTM_CHUNK
    return pl.pallas_call(
        functools.partial(_mlstm_kernel, n_sub=n_sub),
        grid=(S // L,),
        in_specs=[pl.BlockSpec((L, GROUP), lambda c: (c, P_MQ)),
                  pl.BlockSpec((L, GROUP), lambda c: (c, P_MK)),
                  pl.BlockSpec((L, GROUP), lambda c: (c, P_MV)),
                  pl.BlockSpec((L, GROUP), lambda c: (c, P_MO)),
                  pl.BlockSpec((L, LANES), lambda c: (c, 0)),
                  pl.BlockSpec((LANES, L), lambda c: (0, c)),
                  pl.BlockSpec((1, LANES), lambda c: (0, 0)),
                  pl.BlockSpec((LANES, LANES), lambda c: (0, 0)),
                  pl.BlockSpec((1, GROUP), lambda c: (0, 0))],
        out_specs=pl.BlockSpec((L, GROUP), lambda c: (c, 0)),
        out_shape=jax.ShapeDtypeStruct((S, GROUP), BF16),
        scratch_shapes=[pltpu.VMEM((HEADS, HEAD_DIM, 2 * HEAD_DIM), F32), pltpu.VMEM((8, LANES), F32)],
        compiler_params=_params("arbitrary"),
        name="mlstm",
    )(pb, pb, pb, pf, gates, gates_t, bias_row, bias_col, head_gain)


def _softplus(z):
    return jnp.maximum(z, 0.0) + jnp.log(1.0 + jnp.exp(-jnp.abs(z)))


def _sb_kernel(q_ref, k_ref, v_ref, u_ref, y_ref, acc_ref, carry_ref, *, tq, tk, heads):
    i = pl.program_id(1)
    dh = HEAD_DIM
    acc_ref[...] = jnp.zeros_like(acc_ref)
    carry_ref[...] = jnp.zeros_like(carry_ref)
    q_pos = i * tq + lax.broadcasted_iota(jnp.int32, (tq, tk), 0)
    k_off = lax.broadcasted_iota(jnp.int32, (tq, tk), 1)

    def block(j, masked):
        rows = pl.ds(pl.multiple_of(j * tk, tk), tk)
        if masked:
            past = (j * tk + k_off) < q_pos
        for hd in range(heads):
            cols = slice(hd * dh, (hd + 1) * dh)
            z = lax.dot_general(q_ref[:, cols], k_ref[rows, cols], (((1,), (1,)), ((), ())),
                                preferred_element_type=F32)
            sp = _softplus(z)
            if masked:
                sp = jnp.where(past, sp, 0.0)
            hi = sp.astype(BF16)
            lo = (sp - hi.astype(F32)).astype(BF16)
            R = (jnp.dot(hi, u_ref[...], preferred_element_type=F32)
                 + jnp.dot(lo, u_ref[...], preferred_element_type=F32))
            carry = carry_ref[hd]
            A = jnp.exp(z - R - carry)
            if masked:
                A = jnp.where(past, A, 0.0)
            acc_ref[:, cols] += jnp.dot(A.astype(BF16), v_ref[rows, cols], preferred_element_type=F32)
            carry_ref[hd] = carry + R[:, 0:1]

    n_diag = tq // tk
    j_hi = (i + 1) * n_diag
    for d in range(n_diag):
        block(j_hi - 1 - d, True)

    def cond(state):
        jj, live = state
        return jnp.logical_and(jj < j_hi - n_diag, live)

    def body(state):
        jj, _ = state
        block(j_hi - n_diag - 1 - jj, False)
        return jj + 1, jnp.min(carry_ref[...]) < SB_DEAD_CARRY

    lax.while_loop(cond, body, (jnp.int32(0), jnp.min(carry_ref[...]) < SB_DEAD_CARRY))
    y_ref[...] = acc_ref[...].astype(BF16)


def _sb_attention(pb, tri_u, *, tq=512, tk=256, heads=2):
    S = pb.shape[0]
    tq = min(tq, S)
    tk = min(tk, tq)
    w = heads * HEAD_DIM
    groups = HEADS // heads
    qcol, kcol, vcol = P_SQ * groups, P_SK * groups, P_SV * groups
    return pl.pallas_call(
        functools.partial(_sb_kernel, tq=tq, tk=tk, heads=heads),
        grid=(groups, S // tq),
        in_specs=[pl.BlockSpec((tq, w), lambda h, i: (i, qcol + h)),
                  pl.BlockSpec((S, w), lambda h, i: (0, kcol + h)),
                  pl.BlockSpec((S, w), lambda h, i: (0, vcol + h)),
                  pl.BlockSpec((tk, tk), lambda h, i: (0, 0))],
        out_specs=pl.BlockSpec((tq, w), lambda h, i: (i, h)),
        out_shape=jax.ShapeDtypeStruct((S, GROUP), BF16),
        scratch_shapes=[pltpu.VMEM((tq, w), F32), pltpu.VMEM((heads, tq, 1), F32)],
        compiler_params=_params("arbitrary", "arbitrary"),
        name="sb_attention",
    )(pb, pb, pb, tri_u)


def _pack_w_in_kernel(wt_ref, main_ref):
    main_ref[...] = wt_ref[0].T.astype(BF16)


def _pack_w_in(w_in):
    depth, D, N = w_in.shape
    G = GROUP
    gate0 = 7 * G
    after = gate0 + 2 * HEADS
    starts = (0, G, 2 * G, 6 * G, after,
              3 * G, 4 * G, 5 * G,
              after + G, after + 2 * G, after + 3 * G)
    w_t = jnp.swapaxes(w_in, 1, 2)

    def panel_start(p):
        s = jnp.int32(starts[0] // 8)
        for idx in range(1, len(starts)):
            s = jnp.where(p == idx, starts[idx] // 8, s)
        return s * 8

    main = pl.pallas_call(
        _pack_w_in_kernel,
        grid=(depth, len(starts)),
        in_specs=[pl.BlockSpec((pl.Element(1), pl.Element(G), pl.Element(D)),
                               lambda l, p: (l, panel_start(p), 0))],
        out_specs=pl.BlockSpec((None, D, G), lambda l, p: (l, 0, p)),
        out_shape=jax.ShapeDtypeStruct((depth, D, len(starts) * G), BF16),
        compiler_params=_params("arbitrary", "arbitrary"),
        name="pack_w_in",
    )(w_t)
    gate_t = jnp.pad(w_t[:, gate0:after, :], ((0, 0), (0, LANES - 2 * HEADS), (0, 0))).astype(BF16)
    return main, jnp.swapaxes(gate_t, 1, 2), gate_t


def _token_mix(h, x, layer, w_main, w_gate, w_gate_t, w_out, conv_w, pool_w, pool_scale, i_bias, f_bias,
               head_gain, g_post, g_next, tri_u):
    pf, pb, gates, gates_t = _in_proj(h, w_main, w_gate, w_gate_t, layer)
    y_conv, y_pool = _conv_pool(pf, conv_w, pool_w, pool_scale[None, :])
    bias = jnp.pad(jnp.concatenate([i_bias, f_bias]), (0, LANES - 2 * HEADS))
    bias_col = jnp.broadcast_to(bias[:, None], (LANES, LANES))
    y_mlstm = _mlstm(pf, pb, gates, gates_t, bias[None, :], bias_col, head_gain[None, :])
    y_sb = _sb_attention(pb, tri_u)
    return _proj_res([y_conv, y_mlstm, y_pool, y_sb], w_out, layer, x, g_post, g_next, res_scale=1.0)


def kernel(x, w_in, w_out, conv_w, pool_w, pool_scale, mlstm_i_bias, mlstm_f_bias, mlstm_head_gain,
           ffn1_w_gate, ffn1_w_up, ffn1_w_down, ffn2_w_gate, ffn2_w_up, ffn2_w_down, norm_gains):
    B, S, D = x.shape
    depth = w_in.shape[0]
    outs = []
    tk = min(256, S)
    tri_u = (jnp.arange(tk)[:, None] >= jnp.arange(tk)[None, :]).astype(BF16)
    w_main, w_gate, w_gate_t = _pack_w_in(w_in)
    w_out_bf = w_out.astype(BF16)
    wd1_bf = ffn1_w_down.astype(BF16)
    wd2_bf = ffn2_w_down.astype(BF16)
    for b in range(B):
        xb = x[b]
        h = _first_norm(xb, norm_gains[0, 0][None, :])
        for l in range(depth):
            g = norm_gains[l]
            g_after = norm_gains[l + 1, 0] if l + 1 < depth else g[0]
            act = _ffn_up(h, ffn1_w_gate, ffn1_w_up, l)
            xb, h = _proj_res([act], wd1_bf, l, xb, g[1][None, :], g[2][None, :], res_scale=0.5, tk=2816)
            xb, h = _token_mix(h, xb, l, w_main, w_gate, w_gate_t, w_out_bf, conv_w[l], pool_w[l],
                               pool_scale[l], mlstm_i_bias[l], mlstm_f_bias[l], mlstm_head_gain[l],
                               g[3][None, :], g[4][None, :], tri_u)
            act = _ffn_up(h, ffn2_w_gate, ffn2_w_up, l)
            xb, h = _proj_res([act], wd2_bf, l, xb, g[5][None, :], g_after[None, :], res_scale=0.5, tk=2816)
        outs.append(xb)
    return jnp.stack(outs, axis=0)
```

```python
import functools

import jax
import jax.numpy as jnp
from jax import lax
from jax.experimental import pallas as pl
from jax.experimental.pallas import tpu as pltpu

F32 = jnp.float32
BF16 = jnp.bfloat16

NORM_EPS = 1e-6
HEAD_DIM = 128
HEADS = 4
GROUP = HEADS * HEAD_DIM
CONV_WIDTH = 3
POOL_WINDOWS = (2, 4, 8, 16)
POOL_HALO = 16
MLSTM_CHUNK = 128
LANES = 128
NEG_BIG = -1e30
SB_DEAD_CARRY = 105.0

VMEM_LIMIT = 56 * 1024 * 1024

F32_PANELS = 5
BF16_PANELS = 6
P_CB, P_CC, P_CU, P_MO, P_PU = range(F32_PANELS)
P_MQ, P_MK, P_MV, P_SQ, P_SK, P_SV = range(BF16_PANELS)


def _params(*sem):
    return pltpu.CompilerParams(dimension_semantics=sem, vmem_limit_bytes=VMEM_LIMIT)


def _rms_rows(y, g):
    ms = jnp.mean(y * y, axis=-1, keepdims=True)
    return y * lax.rsqrt(ms + NORM_EPS) * g


def _split3(x):
    hi = x.astype(BF16)
    r1 = x - hi.astype(F32)
    mid = r1.astype(BF16)
    lo = (r1 - mid.astype(F32)).astype(BF16)
    return hi, mid, lo


def _norm_kernel(x_ref, g_ref, h_ref):
    h_ref[...] = _rms_rows(x_ref[...], g_ref[...]).astype(BF16)


def _first_norm(x, g):
    S, D = x.shape
    tm = 256
    return pl.pallas_call(
        _norm_kernel,
        grid=(S // tm,),
        in_specs=[pl.BlockSpec((tm, D), lambda i: (i, 0)),
                  pl.BlockSpec((1, D), lambda i: (0, 0))],
        out_specs=pl.BlockSpec((tm, D), lambda i: (i, 0)),
        out_shape=jax.ShapeDtypeStruct((S, D), BF16),
        compiler_params=_params("arbitrary"),
        name="first_norm",
    )(x, g)


def _ffn_up_kernel(h_ref, wg_ref, wu_ref, o_ref, wg_bf, wu_bf, *, row_chunk):
    @pl.when(pl.program_id(1) == 0)
    def _():
        wg_bf[...] = wg_ref[...].astype(BF16)
        wu_bf[...] = wu_ref[...].astype(BF16)

    def body(r, carry):
        rows = pl.ds(pl.multiple_of(r * row_chunk, row_chunk), row_chunk)
        h = h_ref[rows, :]
        g = jnp.dot(h, wg_bf[...], preferred_element_type=F32)
        u = jnp.dot(h, wu_bf[...], preferred_element_type=F32)
        o_ref[rows, :] = (g / (1.0 + jnp.exp(-g)) * u).astype(BF16)
        return carry

    lax.fori_loop(0, h_ref.shape[0] // row_chunk, body, 0)


def _ffn_up(h, wg, wu, layer, *, tm=2048, tn=512, row_chunk=1024):
    S, D = h.shape
    N = wg.shape[2]
    tm = min(tm, S)
    return pl.pallas_call(
        functools.partial(_ffn_up_kernel, row_chunk=min(row_chunk, tm)),
        grid=(N // tn, S // tm),
        in_specs=[pl.BlockSpec((tm, D), lambda j, i: (i, 0)),
                  pl.BlockSpec((None, D, tn), lambda j, i: (layer, 0, j)),
                  pl.BlockSpec((None, D, tn), lambda j, i: (layer, 0, j))],
        out_specs=pl.BlockSpec((tm, tn), lambda j, i: (i, j)),
        out_shape=jax.ShapeDtypeStruct((S, N), BF16),
        scratch_shapes=[pltpu.VMEM((D, tn), BF16), pltpu.VMEM((D, tn), BF16)],
        compiler_params=_params("arbitrary", "arbitrary"),
        name="ffn_up",
    )(h, wg, wu)


def _proj_res_kernel(*refs, n_lhs, nk, res_scale, row_chunk, ep_chunk):
    a_refs = refs[:n_lhs]
    w_ref, x_ref, gp_ref, gn_ref, xo_ref, ho_ref = refs[n_lhs:]
    k = pl.program_id(1)
    tm = xo_ref.shape[0]
    kw = a_refs[0].shape[1]

    def accumulate(first):
        def body(r, carry):
            rows = pl.ds(pl.multiple_of(r * row_chunk, row_chunk), row_chunk)
            part = jnp.dot(a_refs[0][rows, :], w_ref[0:kw, :], preferred_element_type=F32)
            for g in range(1, n_lhs):
                part = part + jnp.dot(a_refs[g][rows, :], w_ref[g * kw:(g + 1) * kw, :],
                                      preferred_element_type=F32)
            if first:
                xo_ref[rows, :] = part
            else:
                xo_ref[rows, :] += part
            return carry
        lax.fori_loop(0, tm // row_chunk, body, 0)

    def epilogue():
        def body(r, carry):
            rows = pl.ds(pl.multiple_of(r * ep_chunk, ep_chunk), ep_chunk)
            xn = x_ref[rows, :] + res_scale * _rms_rows(xo_ref[rows, :], gp_ref[...])
            xo_ref[rows, :] = xn
            ho_ref[rows, :] = _rms_rows(xn, gn_ref[...]).astype(BF16)
            return carry
        lax.fori_loop(0, tm // ep_chunk, body, 0, unroll=2)

    def last_step(first):
        half = tm // 2
        for c in range(2):
            rows = slice(c * half, (c + 1) * half)
            part = jnp.dot(a_refs[0][rows, :], w_ref[0:kw, :], preferred_element_type=F32)
            for g in range(1, n_lhs):
                part = part + jnp.dot(a_refs[g][rows, :], w_ref[g * kw:(g + 1) * kw, :],
                                      preferred_element_type=F32)
            xo_ref[rows, :] = part if first else xo_ref[rows, :] + part
            for e in range(half // ep_chunk):
                er = slice(c * half + e * ep_chunk, c * half + (e + 1) * ep_chunk)
                xn = x_ref[er, :] + res_scale * _rms_rows(xo_ref[er, :], gp_ref[...])
                xo_ref[er, :] = xn
                ho_ref[er, :] = _rms_rows(xn, gn_ref[...]).astype(BF16)

    if nk == 1:
        last_step(True)
    else:
        pl.when(k == 0)(lambda: accumulate(True))
        pl.when(jnp.logical_and(k > 0, k < nk - 1))(lambda: accumulate(False))
        pl.when(k == nk - 1)(lambda: last_step(False))


def _proj_res(lhs, w, layer, x, g_post, g_next, *, res_scale, tm=512, tk=1408, row_chunk=512, ep_chunk=64):
    S, D = x.shape
    tm = min(tm, S)
    n_lhs = len(lhs)
    kw = lhs[0].shape[1]
    if n_lhs > 1:
        tk = n_lhs * kw
    nk = w.shape[1] // tk
    return pl.pallas_call(
        functools.partial(_proj_res_kernel, n_lhs=n_lhs, nk=nk, res_scale=res_scale,
                          row_chunk=min(row_chunk, tm), ep_chunk=ep_chunk),
        grid=(S // tm, nk),
        in_specs=[pl.BlockSpec((tm, tk // n_lhs), lambda i, k: (i, k)) for _ in lhs] + [
            pl.BlockSpec((None, tk, D), lambda i, k: (layer, k, 0)),
            pl.BlockSpec((tm, D), lambda i, k: (i, 0)),
            pl.BlockSpec((1, D), lambda i, k: (0, 0)),
            pl.BlockSpec((1, D), lambda i, k: (0, 0))],
        out_specs=[pl.BlockSpec((tm, D), lambda i, k: (i, 0)),
                   pl.BlockSpec((tm, D), lambda i, k: (i, 0))],
        out_shape=[jax.ShapeDtypeStruct((S, D), F32), jax.ShapeDtypeStruct((S, D), BF16)],
        compiler_params=_params("arbitrary", "arbitrary"),
        name="proj_res",
    )(*lhs, w, x, g_post, g_next)


def _in_proj_kernel(h_ref, w_ref, wg_ref, wgt_ref, brow_ref, bcol_ref,
                    of_ref, ob_ref, g_ref, gt_ref, bc_ref, br_ref, *, scale):
    h = h_ref[...]
    for p in range(F32_PANELS):
        cols = slice(p * GROUP, (p + 1) * GROUP)
        of_ref[:, cols] = jnp.dot(h, w_ref[:, cols], preferred_element_type=F32)
    for p in range(BF16_PANELS):
        cols = slice(p * GROUP, (p + 1) * GROUP)
        wcols = slice((F32_PANELS + p) * GROUP, (F32_PANELS + p + 1) * GROUP)
        r = jnp.dot(h, w_ref[:, wcols], preferred_element_type=F32)
        if p in (P_MK, P_SQ):
            r = r * scale
        ob_ref[:, cols] = r.astype(BF16)
    G = jnp.dot(h, wg_ref[...], preferred_element_type=F32) + brow_ref[...]
    GT = (lax.dot_general(wgt_ref[...], h, (((1,), (1,)), ((), ())), preferred_element_type=F32)
          + bcol_ref[:, 0:1])
    g_ref[...] = G
    gt_ref[...] = GT
    L = MLSTM_CHUNK
    row = lax.broadcasted_iota(jnp.int32, (L, L), 0)
    col = lax.broadcasted_iota(jnp.int32, (L, L), 1)
    tri = (col <= row).astype(BF16)
    tri_t = (row <= col).astype(BF16)
    for c in range(h_ref.shape[0] // L):
        rows = slice(c * L, (c + 1) * L)
        bc_ref[rows, :] = sum(jnp.dot(tri, part, preferred_element_type=F32)
                              for part in _split3(_log_sigmoid(G[rows, :])))
        br_ref[:, rows] = sum(jnp.dot(part, tri_t, preferred_element_type=F32)
                              for part in _split3(_log_sigmoid(GT[:, rows])))


def _in_proj(h, w_main, w_gate, w_gate_t, layer, bias_row, bias_col, *, tm=512):
    S, D = h.shape
    tm = min(tm, S)
    nf, nb = F32_PANELS * GROUP, BF16_PANELS * GROUP
    by_rows = pl.BlockSpec((tm, LANES), lambda i: (i, 0))
    by_cols = pl.BlockSpec((LANES, tm), lambda i: (0, i))
    return pl.pallas_call(
        functools.partial(_in_proj_kernel, scale=HEAD_DIM ** -0.5),
        grid=(S // tm,),
        in_specs=[pl.BlockSpec((tm, D), lambda i: (i, 0)),
                  pl.BlockSpec((None, D, nf + nb), lambda i: (layer, 0, 0), pipeline_mode=pl.Buffered(1)),
                  pl.BlockSpec((None, D, LANES), lambda i: (layer, 0, 0)),
                  pl.BlockSpec((None, LANES, D), lambda i: (layer, 0, 0)),
                  pl.BlockSpec((1, LANES), lambda i: (0, 0)),
                  pl.BlockSpec((LANES, LANES), lambda i: (0, 0))],
        out_specs=[pl.BlockSpec((tm, nf), lambda i: (i, 0)),
                   pl.BlockSpec((tm, nb), lambda i: (i, 0)),
                   by_rows, by_cols, by_rows, by_cols],
        out_shape=[jax.ShapeDtypeStruct((S, nf), F32),
                   jax.ShapeDtypeStruct((S, nb), BF16),
                   jax.ShapeDtypeStruct((S, LANES), F32),
                   jax.ShapeDtypeStruct((LANES, S), F32),
                   jax.ShapeDtypeStruct((S, LANES), F32),
                   jax.ShapeDtypeStruct((LANES, S), F32)],
        compiler_params=_params("arbitrary"),
        name="in_proj",
    )(h, w_main, w_gate, w_gate_t, bias_row, bias_col)


def _conv_pool_kernel(cb_ref, cc_ref, cu_ref, pu_ref, cch_ref, cuh_ref, puh_ref,
                      cw_ref, pw_ref, ps_ref, yc_ref, yp_ref, zbuf, pbuf):
    i = pl.program_id(0)
    tm = cb_ref.shape[0]
    keep = (i > 0).astype(F32)
    H = POOL_HALO

    zbuf[0:H, :] = cch_ref[...] * cuh_ref[...] * keep
    zbuf[H:H + tm, :] = cc_ref[...] * cu_ref[...]
    y = cw_ref[0:1, :] * zbuf[H - 2:H - 2 + tm, :]
    y = y + cw_ref[1:2, :] * zbuf[H - 1:H - 1 + tm, :]
    y = y + cw_ref[2:3, :] * zbuf[H:H + tm, :]
    yc_ref[...] = (cb_ref[...] * y).astype(BF16)

    pbuf[0:H, :] = puh_ref[...] * keep
    pbuf[H:H + tm, :] = pu_ref[...]
    t = i * tm + lax.broadcasted_iota(jnp.int32, (tm, 1), 0)
    for g, win in enumerate(POOL_WINDOWS):
        cols = slice(g * LANES, (g + 1) * LANES)
        u = pbuf[H:H + tm, cols]
        wsum = u
        for d in range(1, win):
            wsum = wsum + pbuf[H - d:H - d + tm, cols]
        count = jnp.minimum(t + 1, win).astype(F32)
        pooled = wsum / count - u
        yg = jnp.dot(pooled.astype(BF16), pw_ref[g].astype(BF16), preferred_element_type=F32)
        yp_ref[:, cols] = (yg * ps_ref[:, cols]).astype(BF16)


def _conv_pool(pf, conv_w, pool_w, pool_scale, *, tm=512):
    S = pf.shape[0]
    tm = min(tm, S)
    H = POOL_HALO
    hb = tm // H

    def cur(p):
        return pl.BlockSpec((tm, GROUP), lambda i, p=p: (i, p))

    def halo(p):
        return pl.BlockSpec((H, GROUP), lambda i, p=p: (jnp.maximum(i * hb - 1, 0), p))

    return pl.pallas_call(
        _conv_pool_kernel,
        grid=(S // tm,),
        in_specs=[cur(P_CB), cur(P_CC), cur(P_CU), cur(P_PU), halo(P_CC), halo(P_CU), halo(P_PU),
                  pl.BlockSpec((CONV_WIDTH, GROUP), lambda i: (0, 0)),
                  pl.BlockSpec(pool_w.shape, lambda i: (0, 0, 0)),
                  pl.BlockSpec((1, GROUP), lambda i: (0, 0))],
        out_specs=[pl.BlockSpec((tm, GROUP), lambda i: (i, 0)),
                   pl.BlockSpec((tm, GROUP), lambda i: (i, 0))],
        out_shape=[jax.ShapeDtypeStruct((S, GROUP), BF16), jax.ShapeDtypeStruct((S, GROUP), BF16)],
        scratch_shapes=[pltpu.VMEM((H + tm, GROUP), F32), pltpu.VMEM((H + tm, GROUP), F32)],
        compiler_params=_params("arbitrary"),
        name="conv_pool",
    )(pf, pf, pf, pf, pf, pf, pf, conv_w, pool_w, pool_scale)


def _log_sigmoid(x):
    return jnp.minimum(x, 0.0) - jnp.log(1.0 + jnp.exp(-jnp.abs(x)))


def _mlstm_kernel(q_ref, k_ref, v_ref, o_ref, g_ref, gt_ref, bc_ref, br_ref, gain_ref,
                  y_ref, c_ref, m_ref, *, n_sub):
    L = MLSTM_CHUNK
    dh = HEAD_DIM

    @pl.when(pl.program_id(0) == 0)
    def _():
        c_ref[...] = jnp.zeros_like(c_ref)
        m_ref[...] = jnp.zeros_like(m_ref)

    row = lax.broadcasted_iota(jnp.int32, (L, L), 0)
    col = lax.broadcasted_iota(jnp.int32, (L, L), 1)
    causal = col <= row
    ones_col = (lax.broadcasted_iota(jnp.int32, (L, dh), 1) == 0).astype(BF16)

    for sub in range(n_sub):
        rows = slice(sub * L, (sub + 1) * L)
        G = g_ref[rows, :]
        GT = gt_ref[:, rows]
        bc = bc_ref[rows, :]
        br = br_ref[:, rows]

        y_new, c_new, m_new_rows = [], [], []
        for h in range(HEADS):
            cols = slice(h * dh, (h + 1) * dh)
            q = q_ref[rows, cols]
            k = k_ref[rows, cols]
            v = v_ref[rows, cols]
            b_col = bc[:, HEADS + h:HEADS + h + 1]
            b_row = br[HEADS + h:HEADS + h + 1, :]
            ig_col = G[:, h:h + 1]
            ig_row = GT[h:h + 1, :]
            m_prev = m_ref[h:h + 1, 0:1]
            c_ext = c_ref[h]

            D = jnp.where(causal, b_col - b_row + ig_row, NEG_BIG)
            inter = b_col + m_prev
            m_t = jnp.maximum(inter, jnp.max(D, axis=1, keepdims=True))
            w_intra = jnp.exp(D - m_t)
            w_inter = jnp.exp(inter - m_t)
            qk = lax.dot_general(q, k, (((1,), (1,)), ((), ())), preferred_element_type=F32)
            s = qk * w_intra
            qc = jnp.dot(q, c_ext.astype(BF16), preferred_element_type=F32)
            num = jnp.dot(s.astype(BF16), v, preferred_element_type=F32) + w_inter * qc[:, :dh]
            den = jnp.sum(s, axis=1, keepdims=True) + w_inter * qc[:, dh:dh + 1]
            hh = num / jnp.maximum(jnp.abs(den), jnp.exp(-m_t))
            hh = hh * lax.rsqrt(jnp.mean(hh * hh, axis=1, keepdims=True) + NORM_EPS)
            o = o_ref[rows, cols]
            y_new.append((hh * gain_ref[:, cols] / (1.0 + jnp.exp(-o))).astype(BF16))

            b_last = b_col[L - 1:L, :]
            g_col = b_last - b_col + ig_col
            m_new = jnp.maximum(b_last + m_prev, jnp.max(g_col, axis=0, keepdims=True))
            decay = jnp.exp(b_last + m_prev - m_new)
            wk = jnp.exp(g_col - m_new)
            kw_t = (wk * k.astype(F32)).T.astype(BF16)
            v_ext = jnp.concatenate([v, ones_col], axis=1)
            c_new.append(decay * c_ext + jnp.dot(kw_t, v_ext, preferred_element_type=F32))
            m_new_rows.append(jnp.broadcast_to(m_new, (1, LANES)))

        y_ref[rows, :] = jnp.concatenate(y_new, axis=1)
        for h in range(HEADS):
            c_ref[h] = c_new[h]
            m_ref[h:h + 1, :] = m_new_rows[h]


def _mlstm(pf, pb, gates, gates_t, bsum, bsum_t, head_gain, *, n_sub=1):
    S = pf.shape[0]
    L = n_sub * MLSTM_CHUNK
    return pl.pallas_call(
        functools.partial(_mlstm_kernel, n_sub=n_sub),
        grid=(S // L,),
        in_specs=[pl.BlockSpec((L, GROUP), lambda c: (c, P_MQ)),
                  pl.BlockSpec((L, GROUP), lambda c: (c, P_MK)),
                  pl.BlockSpec((L, GROUP), lambda c: (c, P_MV)),
                  pl.BlockSpec((L, GROUP), lambda c: (c, P_MO)),
                  pl.BlockSpec((L, LANES), lambda c: (c, 0)),
                  pl.BlockSpec((LANES, L), lambda c: (0, c)),
                  pl.BlockSpec((L, LANES), lambda c: (c, 0)),
                  pl.BlockSpec((LANES, L), lambda c: (0, c)),
                  pl.BlockSpec((1, GROUP), lambda c: (0, 0))],
        out_specs=pl.BlockSpec((L, GROUP), lambda c: (c, 0)),
        out_shape=jax.ShapeDtypeStruct((S, GROUP), BF16),
        scratch_shapes=[pltpu.VMEM((HEADS, HEAD_DIM, 2 * HEAD_DIM), F32), pltpu.VMEM((8, LANES), F32)],
        compiler_params=_params("arbitrary"),
        name="mlstm",
    )(pb, pb, pb, pf, gates, gates_t, bsum, bsum_t, head_gain)


def _softplus(z):
    return jnp.maximum(z, 0.0) + jnp.log(1.0 + jnp.exp(-jnp.abs(z)))


def _sb_kernel(q_ref, k_ref, v_ref, u_ref, y_ref, acc_ref, carry_ref, *, tq, tk, heads):
    i = pl.program_id(1)
    dh = HEAD_DIM
    acc_ref[...] = jnp.zeros_like(acc_ref)
    carry_ref[...] = jnp.zeros_like(carry_ref)
    q_pos = i * tq + lax.broadcasted_iota(jnp.int32, (tq, tk), 0)
    k_off = lax.broadcasted_iota(jnp.int32, (tq, tk), 1)

    def block(j, masked):
        rows = pl.ds(pl.multiple_of(j * tk, tk), tk)
        if masked:
            past = (j * tk + k_off) < q_pos
        for hd in range(heads):
            cols = slice(hd * dh, (hd + 1) * dh)
            z = lax.dot_general(q_ref[:, cols], k_ref[rows, cols], (((1,), (1,)), ((), ())),
                                preferred_element_type=F32)
            sp = _softplus(z)
            if masked:
                sp = jnp.where(past, sp, 0.0)
            hi = sp.astype(BF16)
            lo = (sp - hi.astype(F32)).astype(BF16)
            R = (jnp.dot(hi, u_ref[...], preferred_element_type=F32)
                 + jnp.dot(lo, u_ref[...], preferred_element_type=F32))
            carry = carry_ref[hd]
            A = jnp.exp(z - R - carry)
            if masked:
                A = jnp.where(past, A, 0.0)
            acc_ref[:, cols] += jnp.dot(A.astype(BF16), v_ref[rows, cols], preferred_element_type=F32)
            carry_ref[hd] = carry + R[:, 0:1]

    n_diag = tq // tk
    j_hi = (i + 1) * n_diag
    for d in range(n_diag):
        block(j_hi - 1 - d, True)

    def cond(state):
        jj, live = state
        return jnp.logical_and(jj < j_hi - n_diag, live)

    def body(state):
        jj, _ = state
        block(j_hi - n_diag - 1 - jj, False)
        return jj + 1, jnp.min(carry_ref[...]) < SB_DEAD_CARRY

    lax.while_loop(cond, body, (jnp.int32(0), jnp.min(carry_ref[...]) < SB_DEAD_CARRY))
    y_ref[...] = acc_ref[...].astype(BF16)


def _sb_attention(pb, tri_u, *, tq=512, tk=256, heads=2):
    S = pb.shape[0]
    tq = min(tq, S)
    tk = min(tk, tq)
    w = heads * HEAD_DIM
    groups = HEADS // heads
    qcol, kcol, vcol = P_SQ * groups, P_SK * groups, P_SV * groups
    return pl.pallas_call(
        functools.partial(_sb_kernel, tq=tq, tk=tk, heads=heads),
        grid=(groups, S // tq),
        in_specs=[pl.BlockSpec((tq, w), lambda h, i: (i, qcol + h)),
                  pl.BlockSpec((S, w), lambda h, i: (0, kcol + h)),
                  pl.BlockSpec((S, w), lambda h, i: (0, vcol + h)),
                  pl.BlockSpec((tk, tk), lambda h, i: (0, 0))],
        out_specs=pl.BlockSpec((tq, w), lambda h, i: (i, h)),
        out_shape=jax.ShapeDtypeStruct((S, GROUP), BF16),
        scratch_shapes=[pltpu.VMEM((tq, w), F32), pltpu.VMEM((heads, tq, 1), F32)],
        compiler_params=_params("arbitrary", "arbitrary"),
        name="sb_attention",
    )(pb, pb, pb, tri_u)


def _pack_w_in_kernel(wt_ref, main_ref):
    main_ref[...] = wt_ref[0].T.astype(BF16)


def _pack_w_in(w_in):
    depth, D, N = w_in.shape
    G = GROUP
    gate0 = 7 * G
    after = gate0 + 2 * HEADS
    starts = (0, G, 2 * G, 6 * G, after,
              3 * G, 4 * G, 5 * G,
              after + G, after + 2 * G, after + 3 * G)
    w_t = jnp.swapaxes(w_in, 1, 2)

    def panel_start(p):
        s = jnp.int32(starts[0] // 8)
        for idx in range(1, len(starts)):
            s = jnp.where(p == idx, starts[idx] // 8, s)
        return s * 8

    main = pl.pallas_call(
        _pack_w_in_kernel,
        grid=(depth, len(starts)),
        in_specs=[pl.BlockSpec((pl.Element(1), pl.Element(G), pl.Element(D)),
                               lambda l, p: (l, panel_start(p), 0))],
        out_specs=pl.BlockSpec((None, D, G), lambda l, p: (l, 0, p)),
        out_shape=jax.ShapeDtypeStruct((depth, D, len(starts) * G), BF16),
        compiler_params=_params("arbitrary", "arbitrary"),
        name="pack_w_in",
    )(w_t)
    gate_t = jnp.pad(w_t[:, gate0:after, :], ((0, 0), (0, LANES - 2 * HEADS), (0, 0))).astype(BF16)
    return main, jnp.swapaxes(gate_t, 1, 2), gate_t


def _token_mix(h, x, layer, w_main, w_gate, w_gate_t, w_out, conv_w, pool_w, pool_scale, i_bias, f_bias,
               head_gain, g_post, g_next, tri_u):
    bias = jnp.pad(jnp.concatenate([i_bias, f_bias]), (0, LANES - 2 * HEADS))
    bias_col = jnp.broadcast_to(bias[:, None], (LANES, LANES))
    pf, pb, gates, gates_t, bsum, bsum_t = _in_proj(h, w_main, w_gate, w_gate_t, layer,
                                                     bias[None, :], bias_col)
    y_conv, y_pool = _conv_pool(pf, conv_w, pool_w, pool_scale[None, :])
    y_mlstm = _mlstm(pf, pb, gates, gates_t, bsum, bsum_t, head_gain[None, :])
    y_sb = _sb_attention(pb, tri_u)
    return _proj_res([y_conv, y_mlstm, y_pool, y_sb], w_out, layer, x, g_post, g_next, res_scale=1.0)


def kernel(x, w_in, w_out, conv_w, pool_w, pool_scale, mlstm_i_bias, mlstm_f_bias, mlstm_head_gain,
           ffn1_w_gate, ffn1_w_up, ffn1_w_down, ffn2_w_gate, ffn2_w_up, ffn2_w_down, norm_gains):
    B, S, D = x.shape
    depth = w_in.shape[0]
    outs = []
    tk = min(256, S)
    tri_u = (jnp.arange(tk)[:, None] >= jnp.arange(tk)[None, :]).astype(BF16)
    w_main, w_gate, w_gate_t = _pack_w_in(w_in)
    w_out_bf = w_out.astype(BF16)
    wd1_bf = ffn1_w_down.astype(BF16)
    wd2_bf = ffn2_w_down.astype(BF16)
    for b in range(B):
        xb = x[b]
        h = _first_norm(xb, norm_gains[0, 0][None, :])
        for l in range(depth):
            g = norm_gains[l]
            g_after = norm_gains[l + 1, 0] if l + 1 < depth else g[0]
            act = _ffn_up(h, ffn1_w_gate, ffn1_w_up, l)
            xb, h = _proj_res([act], wd1_bf, l, xb, g[1][None, :], g[2][None, :], res_scale=0.5, tk=2816)
            xb, h = _token_mix(h, xb, l, w_main, w_gate, w_gate_t, w_out_bf, conv_w[l], pool_w[l],
                               pool_scale[l], mlstm_i_bias[l], mlstm_f_bias[l], mlstm_head_gain[l],
                               g[3][None, :], g[4][None, :], tri_u)
            act = _ffn_up(h, ffn2_w_gate, ffn2_w_up, l)
            xb, h = _proj_res([act], wd2_bf, l, xb, g[5][None, :], g_after[None, :], res_scale=0.5, tk=2816)
        outs.append(xb)
    return jnp.stack(outs, axis=0)
```
